```python
import jax, jax.numpy as jnp
from jax import lax
import numpy as np

D_MODEL = 4096
BATCH = 1
SEQ = 16384
DEPTH = 4

CHUNK = 64

HEAD_DIM = 128
N_ATT_HEADS = D_MODEL // 256
D_ATT = N_ATT_HEADS * HEAD_DIM
Q_BLOCK = 128
D_RNN = D_MODEL // 2
N_RNN_BLOCKS = 16
RNN_BLOCK = D_RNN // N_RNN_BLOCKS
CONV_WIDTH = 4
RG_LRU_C = 8.0
D_FF = 2 * D_MODEL
N_EXPERTS = 8
TOP_K = 2
D_EXPERT = D_MODEL // 4
LN_EPS = 1e-5
DEEPNORM_ALPHA = (2 * DEPTH) ** 0.25
DEEPNORM_BETA = (8 * DEPTH) ** -0.25
N_DENSE = (DEPTH + 1) // 2
N_MOE = DEPTH // 2
IN_SIZES = (D_ATT, D_ATT, D_ATT, N_ATT_HEADS, D_RNN, D_RNN, D_MODEL, D_MODEL)
N_IN = sum(IN_SIZES)

kernel_name = "fox_rglru_gated_hybrid_deepnorm_moe"


def _split_offsets():
    offs, acc = [], 0
    for s in IN_SIZES[:-1]:
        acc += s
        offs.append(acc)
    return offs


def layer_norm(x, g, b):
    xf = x.astype(jnp.float32)
    mu = jnp.mean(xf, axis=-1, keepdims=True)
    xc = xf - mu
    var = jnp.mean(xc * xc, axis=-1, keepdims=True)
    y = xc * lax.rsqrt(var + LN_EPS)
    return (y * g.astype(jnp.float32) + b.astype(jnp.float32)).astype(x.dtype)


def forgetting_attention(q, k, v, f_logit):
    B, S, _ = q.shape
    H, Dh = N_ATT_HEADS, HEAD_DIM
    q = q.reshape(B, S, H, Dh).transpose(0, 2, 1, 3)
    k = k.reshape(B, S, H, Dh).transpose(0, 2, 1, 3)
    v = v.reshape(B, S, H, Dh).transpose(0, 2, 1, 3)
    log_f = jax.nn.log_sigmoid(f_logit.astype(jnp.float32)).transpose(0, 2, 1)
    c = jnp.cumsum(log_f, axis=-1)
    nb = S // Q_BLOCK
    qb = q.reshape(B, H, nb, Q_BLOCK, Dh).transpose(2, 0, 1, 3, 4)
    cb = c.reshape(B, H, nb, Q_BLOCK).transpose(2, 0, 1, 3)
    pos = jnp.arange(S)
    posb = pos.reshape(nb, Q_BLOCK)
    scale = HEAD_DIM ** -0.5

    def block(args):
        q_blk, c_blk, p_blk = args
        s = jnp.einsum('bhqd,bhkd->bhqk', q_blk, k).astype(jnp.float32) * scale
        s = s + c_blk[..., :, None] - c[:, :, None, :]
        mask = pos[None, :] <= p_blk[:, None]
        s = jnp.where(mask, s, -jnp.inf)
        p = jax.nn.softmax(s, axis=-1)
        return jnp.einsum('bhqk,bhkd->bhqd', p.astype(v.dtype), v)

    out = lax.map(block, (qb, cb, posb))
    return out.transpose(1, 0, 3, 2, 4).reshape(B, S, D_ATT)


def rg_lru_branch(xr, yr, conv_w, conv_b, w_a, b_a, w_x, b_x, lam):
    B, S, _ = xr.shape
    xp = jnp.pad(xr, ((0, 0), (CONV_WIDTH - 1, 0), (0, 0)))
    xc = conv_b
    for j in range(CONV_WIDTH):
        xc = xc + xp[:, j:j + S] * conv_w[j]
    xb = xc.reshape(B, S, N_RNN_BLOCKS, RNN_BLOCK)
    r = jax.nn.sigmoid(jnp.einsum('bsnc,ncd->bsnd', xb, w_a).reshape(B, S, D_RNN) + b_a)
    i = jax.nn.sigmoid(jnp.einsum('bsnc,ncd->bsnd', xb, w_x).reshape(B, S, D_RNN) + b_x)
    log_a = -RG_LRU_C * r.astype(jnp.float32) * jax.nn.softplus(-lam.astype(jnp.float32))
    a = jnp.exp(log_a)
    u = (i * xc).astype(jnp.float32) * jnp.sqrt(-jnp.expm1(2.0 * log_a))

    def combine(left, right):
        a1, b1 = left
        a2, b2 = right
        return a1 * a2, a2 * b1 + b2

    _, h = lax.associative_scan(combine, (a, u), axis=1)
    return jax.nn.gelu(yr) * h.astype(yr.dtype)


def token_mixer(x, w_in, b_f, conv_w, conv_b, w_a, b_a, w_x, b_x, lam, w_att_o, w_rnn_o, w_out):
    proj = x @ w_in
    q, k, v, f, rx, ry, ga, gr = jnp.split(proj, _split_offsets(), axis=-1)
    att = forgetting_attention(q, k, v, f + b_f) @ w_att_o
    rnn = rg_lru_branch(rx, ry, conv_w, conv_b, w_a, b_a, w_x, b_x, lam) @ w_rnn_o
    merged = jax.nn.sigmoid(ga) * att + jax.nn.sigmoid(gr) * rnn
    return merged @ w_out


def dense_swiglu(x, w_in, w_out):
    g, u = jnp.split(x @ w_in, 2, axis=-1)
    return (jax.nn.silu(g) * u) @ w_out


def moe_swiglu(x, w_router, w_in, w_out):
    logits = (x @ w_router).astype(jnp.float32)
    top_v, top_i = lax.top_k(logits, TOP_K)
    top_w = jax.nn.softmax(top_v, axis=-1)
    comb = jnp.sum(jax.nn.one_hot(top_i, N_EXPERTS, dtype=jnp.float32) * top_w[..., None], axis=-2)
    h = jnp.einsum('bsd,edf->bsef', x, w_in)
    g, u = jnp.split(h, 2, axis=-1)
    act = jax.nn.silu(g) * u * comb.astype(x.dtype)[..., None]
    return jnp.einsum('bsef,efd->bsd', act, w_out)


def setup_inputs(seed: int = 0) -> dict:
    key = jax.random.key(seed)
    ks = jax.random.split(key, 24)

    def nrm(k, shape, scale):
        return jax.random.normal(k, shape, jnp.float32) * scale

    u_a = jax.random.uniform(ks[9], (DEPTH, D_RNN), jnp.float32, 0.9, 0.999)
    s_a = u_a ** (1.0 / RG_LRU_C)
    lam = jnp.log(s_a) - jnp.log1p(-s_a)
    return {
        "x": nrm(ks[0], (BATCH, SEQ, D_MODEL), 1.0),
        "w_in": nrm(ks[1], (DEPTH, D_MODEL, N_IN), D_MODEL ** -0.5),
        "b_f": jax.random.uniform(ks[2], (DEPTH, N_ATT_HEADS), jnp.float32, 2.0, 6.0),
        "conv_w": nrm(ks[3], (DEPTH, CONV_WIDTH, D_RNN), CONV_WIDTH ** -0.5),
        "conv_b": nrm(ks[4], (DEPTH, D_RNN), 0.01),
        "w_a": nrm(ks[5], (DEPTH, N_RNN_BLOCKS, RNN_BLOCK, RNN_BLOCK), RNN_BLOCK ** -0.5),
        "b_a": nrm(ks[6], (DEPTH, D_RNN), 0.01),
        "w_x": nrm(ks[7], (DEPTH, N_RNN_BLOCKS, RNN_BLOCK, RNN_BLOCK), RNN_BLOCK ** -0.5),
        "b_x": nrm(ks[8], (DEPTH, D_RNN), 0.01),
        "lam": lam,
        "w_att_o": nrm(ks[10], (DEPTH, D_ATT, D_MODEL), D_ATT ** -0.5),
        "w_rnn_o": nrm(ks[11], (DEPTH, D_RNN, D_MODEL), D_RNN ** -0.5),
        "w_out": nrm(ks[12], (DEPTH, D_MODEL, D_MODEL), DEEPNORM_BETA * D_MODEL ** -0.5),
        "ln1_g": 1.0 + nrm(ks[13], (DEPTH, D_MODEL), 0.02),
        "ln1_b": nrm(ks[14], (DEPTH, D_MODEL), 0.02),
        "w_ffn_in": nrm(ks[15], (N_DENSE, D_MODEL, 2 * D_FF), D_MODEL ** -0.5),
        "w_ffn_out": nrm(ks[16], (N_DENSE, D_FF, D_MODEL), DEEPNORM_BETA * D_FF ** -0.5),
        "w_router": nrm(ks[17], (N_MOE, D_MODEL, N_EXPERTS), D_MODEL ** -0.5),
        "w_exp_in": nrm(ks[18], (N_MOE, N_EXPERTS, D_MODEL, 2 * D_EXPERT), D_MODEL ** -0.5),
        "w_exp_out": nrm(ks[19], (N_MOE, N_EXPERTS, D_EXPERT, D_MODEL), DEEPNORM_BETA * D_EXPERT ** -0.5),
        "ln2_g": 1.0 + nrm(ks[20], (DEPTH, D_MODEL), 0.02),
        "ln2_b": nrm(ks[21], (DEPTH, D_MODEL), 0.02),
    }


def reference(x, w_in, b_f, conv_w, conv_b, w_a, b_a, w_x, b_x, lam, w_att_o, w_rnn_o, w_out,
              ln1_g, ln1_b, w_ffn_in, w_ffn_out, w_router, w_exp_in, w_exp_out, ln2_g, ln2_b):
    for l in range(DEPTH):
        mix = token_mixer(x, w_in[l], b_f[l], conv_w[l], conv_b[l], w_a[l], b_a[l], w_x[l], b_x[l],
                          lam[l], w_att_o[l], w_rnn_o[l], w_out[l])
        x = layer_norm(DEEPNORM_ALPHA * x + mix, ln1_g[l], ln1_b[l])
        j = l // 2
        if l % 2 == 0:
            ff = dense_swiglu(x, w_ffn_in[j], w_ffn_out[j])
        else:
            ff = moe_swiglu(x, w_router[j], w_exp_in[j], w_exp_out[j])
        x = layer_norm(DEEPNORM_ALPHA * x + ff, ln2_g[l], ln2_b[l])
    return x
```

```python
import functools

import numpy as np
import jax
import jax.numpy as jnp
from jax import lax
from jax.experimental import pallas as pl
from jax.experimental.pallas import tpu as pltpu

F32 = jnp.float32
BF16 = jnp.bfloat16

LANES = 128
SUBLANES = 8
V7X_VMEM_BYTES = 64 * 2 ** 20
VMEM_RESERVE_BYTES = 8 * 2 ** 20

CONV_WIDTH = 4
RG_LRU_C = 8.0
TOP_K = 2
LN_EPS = 1e-5
NEG_INF = float("-inf")


def _compiler_params(semantics, block_bytes):
    limit = min(int(block_bytes) + VMEM_RESERVE_BYTES, V7X_VMEM_BYTES - VMEM_RESERVE_BYTES)
    return pltpu.CompilerParams(dimension_semantics=semantics, vmem_limit_bytes=limit)


def _nbytes(shape, dtype):
    return int(np.prod(shape)) * jnp.dtype(dtype).itemsize


def _tile(dim, pref):
    if dim <= pref:
        return dim
    t = pref - pref % LANES
    while dim % t:
        t -= LANES
    assert t > 0, (dim, pref)
    return t


def _mm_body(a_ref, b_ref, o_ref):
    o_ref[...] = jnp.dot(a_ref[...], b_ref[...], preferred_element_type=F32).astype(o_ref.dtype)


def _matmul(a, b, out_dtype, *, tm=1024, tn=512, name):
    m, k = a.shape
    n = b.shape[1]
    tm, tn = _tile(m, tm), _tile(n, tn)
    vmem = 2 * (_nbytes((tm, k), a.dtype) + _nbytes((k, tn), b.dtype) + _nbytes((tm, tn), out_dtype))
    vmem += _nbytes((tm, tn), F32)
    return pl.pallas_call(
        _mm_body,
        grid=(m // tm, n // tn),
        in_specs=[pl.BlockSpec((tm, k), lambda i, j: (i, 0)),
                  pl.BlockSpec((k, tn), lambda i, j: (0, j))],
        out_specs=pl.BlockSpec((tm, tn), lambda i, j: (i, j)),
        out_shape=jax.ShapeDtypeStruct((m, n), out_dtype),
        compiler_params=_compiler_params(("parallel", "arbitrary"), vmem),
        name=name,
    )(a, b)


def _ln_body(x_ref, y_ref, g_ref, b_ref, o_ref, ob_ref, *, alpha):
    z = alpha * x_ref[...] + y_ref[...]
    mu = jnp.mean(z, axis=-1, keepdims=True)
    zc = z - mu
    var = jnp.mean(zc * zc, axis=-1, keepdims=True)
    y = zc * lax.rsqrt(var + LN_EPS)
    o = y * g_ref[...] + b_ref[...]
    o_ref[...] = o
    ob_ref[...] = o.astype(BF16)


def _deepnorm_ln(x, y, g, b, alpha, *, tm=256):
    m, d = x.shape
    tm = _tile(m, tm)
    row = pl.BlockSpec((tm, d), lambda i: (i, 0))
    vec = pl.BlockSpec((1, d), lambda i: (0, 0))
    vmem = 2 * (3 * _nbytes((tm, d), F32) + _nbytes((tm, d), BF16)) + 3 * _nbytes((tm, d), F32)
    return pl.pallas_call(
        functools.partial(_ln_body, alpha=alpha),
        grid=(m // tm,),
        in_specs=[row, row, vec, vec],
        out_specs=[row, row],
        out_shape=[jax.ShapeDtypeStruct((m, d), F32), jax.ShapeDtypeStruct((m, d), BF16)],
        compiler_params=_compiler_params(("parallel",), vmem),
        name="deepnorm_ln",
    )(x, y, g.reshape(1, d), b.reshape(1, d))


def _shift_rows(x, d, fill):
    rows = lax.broadcasted_iota(jnp.int32, x.shape, 0)
    return jnp.where(rows >= d, pltpu.roll(x, d, 0), fill)


def _linear_scan_rows(a, b):
    d = 1
    while d < a.shape[0]:
        b = a * _shift_rows(b, d, 0.0) + b
        a = a * _shift_rows(a, d, 1.0)
        d *= 2
    return a, b


def _cumsum_rows(x):
    d = 1
    while d < x.shape[0]:
        x = x + _shift_rows(x, d, 0.0)
        d *= 2
    return x


def _log_sigmoid(z):
    return jnp.minimum(z, 0.0) - jnp.log1p(jnp.exp(-jnp.abs(z)))


def _softplus(z):
    return jnp.maximum(z, 0.0) + jnp.log1p(jnp.exp(-jnp.abs(z)))


def _decay_body(x_ref, w_ref, bf_ref, o_ref, carry_ref):
    @pl.when(pl.program_id(0) == 0)
    def _():
        carry_ref[...] = jnp.zeros_like(carry_ref)

    z = jnp.dot(x_ref[...], w_ref[...], preferred_element_type=F32) + bf_ref[...]
    c = _cumsum_rows(_log_sigmoid(z)) + carry_ref[0:1, :]
    t = c.shape[0]
    carry_ref[0:1, :] = c[t - 1:t, :]
    o_ref[...] = c.T[:o_ref.shape[0], :]


def _decay_cumsum(xb, w_f, b_f, *, ts):
    s, d = xb.shape
    hp = max(SUBLANES, -(-b_f.shape[0] // SUBLANES) * SUBLANES)
    pad = LANES - b_f.shape[0]
    w_pad = jnp.pad(w_f, ((0, 0), (0, pad)))
    b_pad = jnp.pad(b_f, (0, pad)).reshape(1, LANES)
    vmem = 2 * (_nbytes((ts, d), BF16) + _nbytes((d, LANES), BF16) + _nbytes((hp, ts), F32))
    vmem += 6 * _nbytes((ts, LANES), F32)
    out = pl.pallas_call(
        _decay_body,
        grid=(s // ts,),
        in_specs=[pl.BlockSpec((ts, d), lambda t: (t, 0)),
                  pl.BlockSpec((d, LANES), lambda t: (0, 0)),
                  pl.BlockSpec((1, LANES), lambda t: (0, 0))],
        out_specs=pl.BlockSpec((hp, ts), lambda t: (0, t)),
        out_shape=jax.ShapeDtypeStruct((hp, s), F32),
        scratch_shapes=[pltpu.VMEM((SUBLANES, LANES), F32)],
        compiler_params=_compiler_params(("arbitrary",), vmem),
        name="decay_cumsum",
    )(xb, w_pad, b_pad)
    return out[:b_f.shape[0]]


def _attn_body(q_ref, k_ref, v_ref, c_ref, o_ref, m_ref, l_ref, acc_ref, *, blk, scale):
    i = pl.program_id(1)
    q = q_ref[...]
    reps = blk // LANES

    def step(j, masked):
        rows = pl.ds(pl.multiple_of(j * blk, blk), blk)
        k = k_ref[rows, :]
        v = v_ref[rows, :]
        s = lax.dot_general(q, k, (((1,), (1,)), ((), ())), preferred_element_type=F32) * scale
        s = s - c_ref[pl.ds(j, 1), :]
        if masked:
            qpos = lax.broadcasted_iota(jnp.int32, s.shape, 0)
            kpos = lax.broadcasted_iota(jnp.int32, s.shape, 1)
            s = jnp.where(kpos <= qpos, s, NEG_INF)
        m_prev = m_ref[...]
        m_new = jnp.maximum(m_prev, jnp.max(s, axis=1, keepdims=True))
        p = jnp.exp(s - pltpu.repeat(m_new, reps, 1))
        corr = jnp.exp(m_prev - m_new)
        l_ref[...] = corr * l_ref[...] + jnp.sum(p, axis=1, keepdims=True)
        acc_ref[...] = corr * acc_ref[...] + jnp.dot(p.astype(v.dtype), v, preferred_element_type=F32)
        m_ref[...] = m_new

    m_ref[...] = jnp.full_like(m_ref, NEG_INF)
    l_ref[...] = jnp.zeros_like(l_ref)
    acc_ref[...] = jnp.zeros_like(acc_ref)
    step(i, True)
    lax.fori_loop(0, i, lambda j, carry: (step(j, False), carry)[1], 0)
    o_ref[...] = (acc_ref[...] / l_ref[...]).astype(o_ref.dtype)


def _forgetting_attention(qkv, c, n_heads, *, blk):
    s = qkv.shape[0]
    dh = LANES
    nb = s // blk
    head_col = lambda off: pl.BlockSpec((s, dh), lambda h, i: (0, off + h))
    vmem = 2 * (2 * _nbytes((s, dh), BF16) + 2 * _nbytes((blk, dh), BF16) + _nbytes((nb, blk), F32))
    vmem += 3 * _nbytes((blk, dh), F32) + 4 * _nbytes((blk, blk), F32)
    return pl.pallas_call(
        functools.partial(_attn_body, blk=blk, scale=dh ** -0.5),
        grid=(n_heads, nb),
        in_specs=[pl.BlockSpec((blk, dh), lambda h, i: (i, h)),
                  head_col(n_heads), head_col(2 * n_heads),
                  pl.BlockSpec((None, nb, blk), lambda h, i: (h, 0, 0))],
        out_specs=pl.BlockSpec((blk, dh), lambda h, i: (i, h)),
        out_shape=jax.ShapeDtypeStruct((s, n_heads * dh), BF16),
        scratch_shapes=[pltpu.VMEM((blk, dh), F32)] * 3,
        compiler_params=_compiler_params(("parallel", "arbitrary"), vmem),
        name="forgetting_attention",
    )(qkv, qkv, qkv, c)


def _gelu_tanh(x):
    cdf = 0.5 * (1.0 + jnp.tanh(np.float32(np.sqrt(2.0 / np.pi)) * (x + 0.044715 * (x * x * x))))
    return x * cdf


def _rglru_body(rx_ref, ry_ref, cw_ref, cb_ref, wa_ref, ba_ref, wx_ref, bx_ref, lam_ref,
                o_ref, xbuf_ref, h_ref):
    ts, tc = rx_ref.shape
    halo = SUBLANES

    @pl.when(pl.program_id(1) == 0)
    def _():
        xbuf_ref[0:halo, :] = jnp.zeros((halo, tc), F32)
        h_ref[...] = jnp.zeros_like(h_ref)

    xbuf_ref[halo:halo + ts, :] = rx_ref[...]
    xc = cb_ref[...]
    for j in range(CONV_WIDTH):
        off = halo - (CONV_WIDTH - 1) + j
        xc = xc + xbuf_ref[off:off + ts, :] * cw_ref[j:j + 1, :]
    xbuf_ref[0:halo, :] = xbuf_ref[ts:ts + halo, :]

    xcb = xc.astype(BF16)
    nblk = wa_ref.shape[0]
    rb = tc // nblk
    gate = lambda w_ref: jnp.concatenate(
        [jnp.dot(xcb[:, n * rb:(n + 1) * rb], w_ref[n], preferred_element_type=F32) for n in range(nblk)],
        axis=1)
    r = jax.nn.sigmoid(gate(wa_ref) + ba_ref[...])
    i = jax.nn.sigmoid(gate(wx_ref) + bx_ref[...])
    log_a = -RG_LRU_C * r * _softplus(-lam_ref[...])
    a = jnp.exp(log_a)
    u = (i * xc) * jnp.sqrt(-jnp.tanh(log_a) * (a * a + 1.0))
    a_cum, h = _linear_scan_rows(a, u)
    h = h + a_cum * h_ref[0:1, :]
    h_ref[0:1, :] = h[ts - 1:ts, :]
    o_ref[...] = (_gelu_tanh(ry_ref[...]) * h).astype(o_ref.dtype)


def _rglru_branch(g2, d_rnn, conv_w, conv_b, w_a, b_a, w_x, b_x, lam, *, ts=256, tc=512):
    s = g2.shape[0]
    ts, tc = _tile(s, ts), _tile(d_rnn, tc)
    rb = w_a.shape[-1]
    nblk = tc // rb
    ncb = d_rnn // tc
    vec = lambda v: v.reshape(1, d_rnn)
    vspec = pl.BlockSpec((1, tc), lambda c, t: (0, c))
    wspec = pl.BlockSpec((nblk, rb, rb), lambda c, t: (c, 0, 0))
    vmem = 2 * (2 * _nbytes((ts, tc), F32) + _nbytes((ts, tc), BF16) + 2 * _nbytes((nblk, rb, rb), BF16))
    vmem += 14 * _nbytes((ts, tc), F32)
    return pl.pallas_call(
        _rglru_body,
        grid=(ncb, s // ts),
        in_specs=[pl.BlockSpec((ts, tc), lambda c, t: (t, c)),
                  pl.BlockSpec((ts, tc), lambda c, t: (t, c + ncb)),
                  pl.BlockSpec((CONV_WIDTH, tc), lambda c, t: (0, c)), vspec,
                  wspec, vspec, wspec, vspec, vspec],
        out_specs=pl.BlockSpec((ts, tc), lambda c, t: (t, c)),
        out_shape=jax.ShapeDtypeStruct((s, d_rnn), BF16),
        scratch_shapes=[pltpu.VMEM((ts + SUBLANES, tc), F32), pltpu.VMEM((SUBLANES, tc), F32)],
        compiler_params=_compiler_params(("parallel", "arbitrary"), vmem),
        name="rglru_branch",
    )(g2, g2, conv_w, vec(conv_b), w_a, vec(b_a), w_x, vec(b_x), vec(lam))


def _merge_body(att_ref, wo_a_ref, rnn_ref, wo_r_ref, ga_ref, gr_ref, o_ref):
    att = jnp.dot(att_ref[...], wo_a_ref[...], preferred_element_type=F32)
    rnn = jnp.dot(rnn_ref[...], wo_r_ref[...], preferred_element_type=F32)
    o_ref[...] = (jax.nn.sigmoid(ga_ref[...]) * att + jax.nn.sigmoid(gr_ref[...]) * rnn).astype(o_ref.dtype)


def _merge_branches(att, w_att_o, rnn, w_rnn_o, g2, gate_col0, *, tm=1024, tn=512):
    m, ka = att.shape
    kr = rnn.shape[1]
    d = w_att_o.shape[1]
    tm, tn = _tile(m, tm), _tile(d, tn)
    ga0, gr0 = gate_col0 // tn, (gate_col0 + d) // tn
    vmem = 2 * (_nbytes((tm, ka + kr), BF16) + _nbytes((ka + kr, tn), BF16) + 2 * _nbytes((tm, tn), F32)
                + _nbytes((tm, tn), BF16)) + 3 * _nbytes((tm, tn), F32)
    return pl.pallas_call(
        _merge_body,
        grid=(m // tm, d // tn),
        in_specs=[pl.BlockSpec((tm, ka), lambda i, j: (i, 0)),
                  pl.BlockSpec((ka, tn), lambda i, j: (0, j)),
                  pl.BlockSpec((tm, kr), lambda i, j: (i, 0)),
                  pl.BlockSpec((kr, tn), lambda i, j: (0, j)),
                  pl.BlockSpec((tm, tn), lambda i, j: (i, j + ga0)),
                  pl.BlockSpec((tm, tn), lambda i, j: (i, j + gr0))],
        out_specs=pl.BlockSpec((tm, tn), lambda i, j: (i, j)),
        out_shape=jax.ShapeDtypeStruct((m, d), BF16),
        compiler_params=_compiler_params(("parallel", "arbitrary"), vmem),
        name="merge_branches",
    )(att, w_att_o, rnn, w_rnn_o, g2, g2)


def _glu_body(a_ref, wg_ref, wu_ref, o_ref):
    a = a_ref[...]
    g = jnp.dot(a, wg_ref[...], preferred_element_type=F32)
    u = jnp.dot(a, wu_ref[...], preferred_element_type=F32)
    o_ref[...] = (g * jax.nn.sigmoid(g) * u).astype(o_ref.dtype)


def _glu_comb_body(a_ref, wg_ref, wu_ref, comb_ref, o_ref):
    a = a_ref[...]
    g = jnp.dot(a, wg_ref[...], preferred_element_type=F32)
    u = jnp.dot(a, wu_ref[...], preferred_element_type=F32)
    comb = pltpu.repeat(comb_ref[...], o_ref.shape[1] // LANES, 1)
    o_ref[...] = (g * jax.nn.sigmoid(g) * u * comb).astype(o_ref.dtype)


def _glu_vmem(tm, k, tn):
    return 2 * (_nbytes((tm, k), BF16) + 2 * _nbytes((k, tn), BF16) + _nbytes((tm, tn), BF16)
                + _nbytes((tm, LANES), F32)) + 3 * _nbytes((tm, tn), F32)


def _dense_glu(xb, w_in, *, tm=1024, tn=512):
    m, k = xb.shape
    f = w_in.shape[1] // 2
    tm, tn = _tile(m, tm), _tile(f, tn)
    nj = f // tn
    return pl.pallas_call(
        _glu_body,
        grid=(m // tm, nj),
        in_specs=[pl.BlockSpec((tm, k), lambda i, j: (i, 0)),
                  pl.BlockSpec((k, tn), lambda i, j: (0, j)),
                  pl.BlockSpec((k, tn), lambda i, j: (0, j + nj))],
        out_specs=pl.BlockSpec((tm, tn), lambda i, j: (i, j)),
        out_shape=jax.ShapeDtypeStruct((m, f), BF16),
        compiler_params=_compiler_params(("parallel", "arbitrary"), _glu_vmem(tm, k, tn)),
        name="dense_glu",
    )(xb, w_in, w_in)


def _expert_glu(xb, w_exp_in, comb_rep, *, tm=1024, tn=512):
    m, k = xb.shape
    n_exp, _, f2 = w_exp_in.shape
    f = f2 // 2
    tm, tn = _tile(m, tm), _tile(f, tn)
    r = f // tn
    return pl.pallas_call(
        _glu_comb_body,
        grid=(m // tm, n_exp * r),
        in_specs=[pl.BlockSpec((tm, k), lambda i, j: (i, 0)),
                  pl.BlockSpec((None, k, tn), lambda i, j: (j // r, 0, j % r)),
                  pl.BlockSpec((None, k, tn), lambda i, j: (j // r, 0, j % r + r)),
                  pl.BlockSpec((tm, LANES), lambda i, j: (i, j // r))],
        out_specs=pl.BlockSpec((tm, tn), lambda i, j: (i, j)),
        out_shape=jax.ShapeDtypeStruct((m, n_exp * f), BF16),
        compiler_params=_compiler_params(("parallel", "arbitrary"), _glu_vmem(tm, k, tn)),
        name="expert_glu",
    )(xb, w_exp_in, w_exp_in, comb_rep)


def _router_body(x_ref, w_ref, o_ref, *, n_exp):
    logits = jnp.dot(x_ref[...], w_ref[...], preferred_element_type=F32, precision=lax.Precision.HIGHEST)
    lane = lax.broadcasted_iota(jnp.int32, logits.shape, 1)
    logits = jnp.where(lane < n_exp, logits, NEG_INF)

    def take_top(vals):
        top = jnp.max(vals, axis=1, keepdims=True)
        idx = jnp.min(jnp.where(vals == top, lane, LANES), axis=1, keepdims=True)
        return top, idx

    v1, i1 = take_top(logits)
    v2, i2 = take_top(jnp.where(lane == i1, NEG_INF, logits))
    e2 = jnp.exp(v2 - v1)
    denom = 1.0 + e2
    w1, w2 = 1.0 / denom, e2 / denom
    for e in range(n_exp):
        comb_e = jnp.where(i1 == e, w1, 0.0) + jnp.where(i2 == e, w2, 0.0)
        o_ref[:, e * LANES:(e + 1) * LANES] = jnp.broadcast_to(comb_e, (comb_e.shape[0], LANES))


def _router(x, w_router, *, tm=512):
    m, d = x.shape
    n_exp = w_router.shape[1]
    tm = _tile(m, tm)
    w_pad = jnp.pad(w_router, ((0, 0), (0, LANES - n_exp)))
    vmem = 2 * (_nbytes((tm, d), F32) + _nbytes((d, LANES), F32) + _nbytes((tm, n_exp * LANES), F32))
    vmem += 8 * _nbytes((tm, LANES), F32) + 3 * _nbytes((tm, d), F32)
    return pl.pallas_call(
        functools.partial(_router_body, n_exp=n_exp),
        grid=(m // tm,),
        in_specs=[pl.BlockSpec((tm, d), lambda i: (i, 0)),
                  pl.BlockSpec((d, LANES), lambda i: (0, 0))],
        out_specs=pl.BlockSpec((tm, n_exp * LANES), lambda i: (i, 0)),
        out_shape=jax.ShapeDtypeStruct((m, n_exp * LANES), F32),
        compiler_params=_compiler_params(("parallel",), vmem),
        name="router_top2",
    )(x, w_pad)


def kernel(x, w_in, b_f, conv_w, conv_b, w_a, b_a, w_x, b_x, lam, w_att_o, w_rnn_o, w_out,
           ln1_g, ln1_b, w_ffn_in, w_ffn_out, w_router, w_exp_in, w_exp_out, ln2_g, ln2_b):
    batch, seq, d_model = x.shape
    depth = w_in.shape[0]
    n_heads = b_f.shape[1]
    d_att = w_att_o.shape[1]
    d_rnn = w_rnn_o.shape[1]
    assert d_att == n_heads * LANES and w_a.shape[-1] == LANES
    assert w_in.shape[2] == 3 * d_att + n_heads + 2 * d_rnn + 2 * d_model
    alpha = float((2 * depth) ** 0.25)
    attn_blk = _tile(seq, 512)

    f_col0 = 3 * d_att
    g2_col0 = f_col0 + n_heads
    w_qkv = w_in[:, :, :f_col0].astype(BF16)
    w_f = w_in[:, :, f_col0:g2_col0].astype(BF16)
    w_g2 = w_in[:, :, g2_col0:].astype(BF16)
    w_a, w_x = w_a.astype(BF16), w_x.astype(BF16)
    w_att_o, w_rnn_o, w_out = w_att_o.astype(BF16), w_rnn_o.astype(BF16), w_out.astype(BF16)
    w_ffn_in, w_ffn_out = w_ffn_in.astype(BF16), w_ffn_out.astype(BF16)
    w_exp_in = w_exp_in.astype(BF16)
    w_exp_out = w_exp_out.astype(BF16).reshape(w_exp_out.shape[0], -1, d_model)

    outs = []
    for bi in range(batch):
        xf = x[bi]
        xb = xf.astype(BF16)
        for l in range(depth):
            qkv = _matmul(xb, w_qkv[l], BF16, name="proj_qkv")
            g2 = _matmul(xb, w_g2[l], F32, name="proj_rnn_gates")
            c = _decay_cumsum(xb, w_f[l], b_f[l], ts=attn_blk)
            c = c.reshape(n_heads, seq // attn_blk, attn_blk)
            att = _forgetting_attention(qkv, c, n_heads, blk=attn_blk)
            rnn = _rglru_branch(g2, d_rnn, conv_w[l], conv_b[l], w_a[l], b_a[l], w_x[l], b_x[l], lam[l])
            merged = _merge_branches(att, w_att_o[l], rnn, w_rnn_o[l], g2, 2 * d_rnn)
            mix = _matmul(merged, w_out[l], F32, name="proj_out")
            xf, xb = _deepnorm_ln(xf, mix, ln1_g[l], ln1_b[l], alpha)
            j = l // 2
            if l % 2 == 0:
                act = _dense_glu(xb, w_ffn_in[j])
                ff = _matmul(act, w_ffn_out[j], F32, tm=512, name="ffn_out")
            else:
                comb = _router(xf, w_router[j])
                act = _expert_glu(xb, w_exp_in[j], comb)
                ff = _matmul(act, w_exp_out[j], F32, tm=512, name="expert_out")
            xf, xb = _deepnorm_ln(xf, ff, ln2_g[l], ln2_b[l], alpha)
        outs.append(xf)
    return jnp.stack(outs)
```

```python
import functools

import numpy as np
import jax
import jax.numpy as jnp
from jax import lax
from jax.experimental import pallas as pl
from jax.experimental.pallas import tpu as pltpu

F32 = jnp.float32
BF16 = jnp.bfloat16

LANES = 128
SUBLANES = 8
V7X_VMEM_BYTES = 64 * 2 ** 20
VMEM_RESERVE_BYTES = 8 * 2 ** 20

CONV_WIDTH = 4
RG_LRU_C = 8.0
TOP_K = 2
LN_EPS = 1e-5
NEG_INF = float("-inf")
LOG2_E = float(np.log2(np.e))
HEADS_PER_STEP = 2


def _compiler_params(semantics, block_bytes):
    limit = min(int(block_bytes) + VMEM_RESERVE_BYTES, V7X_VMEM_BYTES - VMEM_RESERVE_BYTES)
    return pltpu.CompilerParams(dimension_semantics=semantics, vmem_limit_bytes=limit)


def _nbytes(shape, dtype):
    return int(np.prod(shape)) * jnp.dtype(dtype).itemsize


def _tile(dim, pref):
    if dim <= pref:
        return dim
    t = pref - pref % LANES
    while dim % t:
        t -= LANES
    assert t > 0, (dim, pref)
    return t


def _mm_body(a_ref, b_ref, o_ref, *, scaled_cols, col_scale):
    acc = jnp.dot(a_ref[...], b_ref[...], preferred_element_type=F32)
    if scaled_cols:
        acc = acc * jnp.where(pl.program_id(1) * o_ref.shape[1] < scaled_cols, col_scale, 1.0)
    o_ref[...] = acc.astype(o_ref.dtype)


def _matmul(a, b, out_dtype, *, tm=1024, tn=512, scaled_cols=0, col_scale=1.0, name):
    m, k = a.shape
    n = b.shape[1]
    tm, tn = _tile(m, tm), _tile(n, tn)
    assert scaled_cols % tn == 0
    vmem = 2 * (_nbytes((tm, k), a.dtype) + _nbytes((k, tn), b.dtype) + _nbytes((tm, tn), out_dtype))
    vmem += _nbytes((tm, tn), F32)
    return pl.pallas_call(
        functools.partial(_mm_body, scaled_cols=scaled_cols, col_scale=col_scale),
        grid=(m // tm, n // tn),
        in_specs=[pl.BlockSpec((tm, k), lambda i, j: (i, 0)),
                  pl.BlockSpec((k, tn), lambda i, j: (0, j))],
        out_specs=pl.BlockSpec((tm, tn), lambda i, j: (i, j)),
        out_shape=jax.ShapeDtypeStruct((m, n), out_dtype),
        compiler_params=_compiler_params(("parallel", "arbitrary"), vmem),
        name=name,
    )(a, b)


def _ln_body(x_ref, y_ref, g_ref, b_ref, o_ref, ob_ref, *, alpha):
    z = alpha * x_ref[...] + y_ref[...]
    mu = jnp.mean(z, axis=-1, keepdims=True)
    zc = z - mu
    var = jnp.mean(zc * zc, axis=-1, keepdims=True)
    y = zc * lax.rsqrt(var + LN_EPS)
    o = y * g_ref[...] + b_ref[...]
    o_ref[...] = o
    ob_ref[...] = o.astype(BF16)


def _deepnorm_ln(x, y, g, b, alpha, *, tm=256):
    m, d = x.shape
    tm = _tile(m, tm)
    row = pl.BlockSpec((tm, d), lambda i: (i, 0))
    vec = pl.BlockSpec((1, d), lambda i: (0, 0))
    vmem = 2 * (3 * _nbytes((tm, d), F32) + _nbytes((tm, d), BF16)) + 3 * _nbytes((tm, d), F32)
    return pl.pallas_call(
        functools.partial(_ln_body, alpha=alpha),
        grid=(m // tm,),
        in_specs=[row, row, vec, vec],
        out_specs=[row, row],
        out_shape=[jax.ShapeDtypeStruct((m, d), F32), jax.ShapeDtypeStruct((m, d), BF16)],
        compiler_params=_compiler_params(("parallel",), vmem),
        name="deepnorm_ln",
    )(x, y, g.reshape(1, d), b.reshape(1, d))


def _shift_rows(x, d, fill):
    rows = lax.broadcasted_iota(jnp.int32, x.shape, 0)
    return jnp.where(rows >= d, pltpu.roll(x, d, 0), fill)


def _linear_scan_rows(a, b):
    d = 1
    while d < a.shape[0]:
        b = a * _shift_rows(b, d, 0.0) + b
        a = a * _shift_rows(a, d, 1.0)
        d *= 2
    return a, b


def _cumsum_rows(x):
    d = 1
    while d < x.shape[0]:
        x = x + _shift_rows(x, d, 0.0)
        d *= 2
    return x


def _log_sigmoid(z):
    return jnp.minimum(z, 0.0) - jnp.log1p(jnp.exp(-jnp.abs(z)))


def _softplus(z):
    return jnp.maximum(z, 0.0) + jnp.log1p(jnp.exp(-jnp.abs(z)))


def _decay_body(x_ref, w_ref, bf_ref, o_ref, carry_ref):
    @pl.when(pl.program_id(0) == 0)
    def _():
        carry_ref[...] = jnp.zeros_like(carry_ref)

    z = jnp.dot(x_ref[...], w_ref[...], preferred_element_type=F32) + bf_ref[...]
    c = _cumsum_rows(_log_sigmoid(z)) + carry_ref[0:1, :]
    t = c.shape[0]
    carry_ref[0:1, :] = c[t - 1:t, :]
    o_ref[...] = (c * LOG2_E).T[:o_ref.shape[0], :]


def _decay_cumsum(xb, w_f, b_f, *, ts):
    s, d = xb.shape
    hp = max(SUBLANES, -(-b_f.shape[0] // SUBLANES) * SUBLANES)
    pad = LANES - b_f.shape[0]
    w_pad = jnp.pad(w_f, ((0, 0), (0, pad)))
    b_pad = jnp.pad(b_f, (0, pad)).reshape(1, LANES)
    vmem = 2 * (_nbytes((ts, d), BF16) + _nbytes((d, LANES), BF16) + _nbytes((hp, ts), F32))
    vmem += 6 * _nbytes((ts, LANES), F32)
    out = pl.pallas_call(
        _decay_body,
        grid=(s // ts,),
        in_specs=[pl.BlockSpec((ts, d), lambda t: (t, 0)),
                  pl.BlockSpec((d, LANES), lambda t: (0, 0)),
                  pl.BlockSpec((1, LANES), lambda t: (0, 0))],
        out_specs=pl.BlockSpec((hp, ts), lambda t: (0, t)),
        out_shape=jax.ShapeDtypeStruct((hp, s), F32),
        scratch_shapes=[pltpu.VMEM((SUBLANES, LANES), F32)],
        compiler_params=_compiler_params(("arbitrary",), vmem),
        name="decay_cumsum",
    )(xb, w_pad, b_pad)
    return out[:b_f.shape[0]]


def _attn_body(q_ref, k_ref, v_ref, c_ref, o_ref, m_ref, l_ref, acc_ref, *, blk):
    i = pl.program_id(1)
    dh = LANES
    n_sub = q_ref.shape[1] // dh
    reps = blk // LANES

    def step(j, a, masked):
        rows = pl.ds(pl.multiple_of(j * blk, blk), blk)
        cols = slice(a * dh, (a + 1) * dh)
        q = q_ref[:, cols]
        k = k_ref[rows, cols]
        v = v_ref[rows, cols]
        s = lax.dot_general(q, k, (((1,), (1,)), ((), ())), preferred_element_type=F32)
        s = s - c_ref[a, pl.ds(j, 1), :]
        if masked:
            qpos = lax.broadcasted_iota(jnp.int32, s.shape, 0)
            kpos = lax.broadcasted_iota(jnp.int32, s.shape, 1)
            s = jnp.where(kpos <= qpos, s, NEG_INF)
        m_prev = m_ref[a]
        m_new = jnp.maximum(m_prev, jnp.max(s, axis=1, keepdims=True))
        p = jnp.exp2(s - pltpu.repeat(m_new, reps, 1))
        corr = jnp.exp2(m_prev - m_new)
        l_ref[a] = corr * l_ref[a] + jnp.sum(p, axis=1, keepdims=True)
        acc_ref[a] = corr * acc_ref[a] + jnp.dot(p.astype(v.dtype), v, preferred_element_type=F32)
        m_ref[a] = m_new

    def all_heads(j, masked):
        for a in range(n_sub):
            step(j, a, masked)

    m_ref[...] = jnp.full_like(m_ref, NEG_INF)
    l_ref[...] = jnp.zeros_like(l_ref)
    acc_ref[...] = jnp.zeros_like(acc_ref)
    all_heads(i, True)

    def pair(jj, carry):
        all_heads(2 * jj, False)
        all_heads(2 * jj + 1, False)
        return carry

    lax.fori_loop(0, i // 2, pair, 0)

    @pl.when(i % 2 == 1)
    def _():
        all_heads(i - 1, False)

    for a in range(n_sub):
        o_ref[:, a * dh:(a + 1) * dh] = (acc_ref[a] / l_ref[a]).astype(o_ref.dtype)


def _forgetting_attention(qkv, c, n_heads, *, blk):
    s = qkv.shape[0]
    dh = LANES
    nb = s // blk
    hps = HEADS_PER_STEP if n_heads % HEADS_PER_STEP == 0 else 1
    w = hps * dh
    groups = n_heads // hps
    head_cols = lambda off: pl.BlockSpec((s, w), lambda h, i: (0, off + h))
    vmem = 2 * (2 * _nbytes((s, w), BF16) + 2 * _nbytes((blk, w), BF16) + _nbytes((hps, nb, blk), F32))
    vmem += 3 * _nbytes((hps, blk, dh), F32) + 4 * hps * _nbytes((blk, blk), F32)
    return pl.pallas_call(
        functools.partial(_attn_body, blk=blk),
        grid=(groups, nb),
        in_specs=[pl.BlockSpec((blk, w), lambda h, i: (i, h)),
                  head_cols(groups), head_cols(2 * groups),
                  pl.BlockSpec((hps, nb, blk), lambda h, i: (h, 0, 0))],
        out_specs=pl.BlockSpec((blk, w), lambda h, i: (i, h)),
        out_shape=jax.ShapeDtypeStruct((s, n_heads * dh), BF16),
        scratch_shapes=[pltpu.VMEM((hps, blk, dh), F32)] * 3,
        compiler_params=_compiler_params(("parallel", "arbitrary"), vmem),
        name="forgetting_attention",
    )(qkv, qkv, qkv, c)


def _gelu_tanh(x):
    cdf = 0.5 * (1.0 + jnp.tanh(np.float32(np.sqrt(2.0 / np.pi)) * (x + 0.044715 * (x * x * x))))
    return x * cdf


def _rglru_body(rx_ref, ry_ref, cw_ref, cb_ref, wa_ref, ba_ref, wx_ref, bx_ref, lam_ref,
                o_ref, xbuf_ref, h_ref):
    ts, tc = rx_ref.shape
    halo = SUBLANES

    @pl.when(pl.program_id(1) == 0)
    def _():
        xbuf_ref[0:halo, :] = jnp.zeros((halo, tc), F32)
        h_ref[...] = jnp.zeros_like(h_ref)

    xbuf_ref[halo:halo + ts, :] = rx_ref[...]
    xc = cb_ref[...]
    for j in range(CONV_WIDTH):
        off = halo - (CONV_WIDTH - 1) + j
        xc = xc + xbuf_ref[off:off + ts, :] * cw_ref[j:j + 1, :]
    xbuf_ref[0:halo, :] = xbuf_ref[ts:ts + halo, :]

    xcb = xc.astype(BF16)
    nblk = wa_ref.shape[0]
    rb = tc // nblk
    gate = lambda w_ref: jnp.concatenate(
        [jnp.dot(xcb[:, n * rb:(n + 1) * rb], w_ref[n], preferred_element_type=F32) for n in range(nblk)],
        axis=1)
    r = jax.nn.sigmoid(gate(wa_ref) + ba_ref[...])
    i = jax.nn.sigmoid(gate(wx_ref) + bx_ref[...])
    log_a = -RG_LRU_C * r * _softplus(-lam_ref[...])
    a = jnp.exp(log_a)
    u = (i * xc) * jnp.sqrt(-jnp.tanh(log_a) * (a * a + 1.0))
    a_cum, h = _linear_scan_rows(a, u)
    h = h + a_cum * h_ref[0:1, :]
    h_ref[0:1, :] = h[ts - 1:ts, :]
    o_ref[...] = (_gelu_tanh(ry_ref[...]) * h).astype(o_ref.dtype)


def _rglru_branch(g2, d_rnn, conv_w, conv_b, w_a, b_a, w_x, b_x, lam, *, ts=256, tc=512):
    s = g2.shape[0]
    ts, tc = _tile(s, ts), _tile(d_rnn, tc)
    rb = w_a.shape[-1]
    nblk = tc // rb
    ncb = d_rnn // tc
    vec = lambda v: v.reshape(1, d_rnn)
    vspec = pl.BlockSpec((1, tc), lambda c, t: (0, c))
    wspec = pl.BlockSpec((nblk, rb, rb), lambda c, t: (c, 0, 0))
    vmem = 2 * (2 * _nbytes((ts, tc), F32) + _nbytes((ts, tc), BF16) + 2 * _nbytes((nblk, rb, rb), BF16))
    vmem += 14 * _nbytes((ts, tc), F32)
    return pl.pallas_call(
        _rglru_body,
        grid=(ncb, s // ts),
        in_specs=[pl.BlockSpec((ts, tc), lambda c, t: (t, c)),
                  pl.BlockSpec((ts, tc), lambda c, t: (t, c + ncb)),
                  pl.BlockSpec((CONV_WIDTH, tc), lambda c, t: (0, c)), vspec,
                  wspec, vspec, wspec, vspec, vspec],
        out_specs=pl.BlockSpec((ts, tc), lambda c, t: (t, c)),
        out_shape=jax.ShapeDtypeStruct((s, d_rnn), BF16),
        scratch_shapes=[pltpu.VMEM((ts + SUBLANES, tc), F32), pltpu.VMEM((SUBLANES, tc), F32)],
        compiler_params=_compiler_params(("parallel", "arbitrary"), vmem),
        name="rglru_branch",
    )(g2, g2, conv_w, vec(conv_b), w_a, vec(b_a), w_x, vec(b_x), vec(lam))


def _merge_body(att_ref, wo_a_ref, rnn_ref, wo_r_ref, ga_ref, gr_ref, o_ref):
    att = jnp.dot(att_ref[...], wo_a_ref[...], preferred_element_type=F32)
    rnn = jnp.dot(rnn_ref[...], wo_r_ref[...], preferred_element_type=F32)
    o_ref[...] = (jax.nn.sigmoid(ga_ref[...]) * att + jax.nn.sigmoid(gr_ref[...]) * rnn).astype(o_ref.dtype)


def _merge_branches(att, w_att_o, rnn, w_rnn_o, g2, gate_col0, *, tm=1024, tn=512):
    m, ka = att.shape
    kr = rnn.shape[1]
    d = w_att_o.shape[1]
    tm, tn = _tile(m, tm), _tile(d, tn)
    ga0, gr0 = gate_col0 // tn, (gate_col0 + d) // tn
    vmem = 2 * (_nbytes((tm, ka + kr), BF16) + _nbytes((ka + kr, tn), BF16) + 2 * _nbytes((tm, tn), F32)
                + _nbytes((tm, tn), BF16)) + 3 * _nbytes((tm, tn), F32)
    return pl.pallas_call(
        _merge_body,
        grid=(m // tm, d // tn),
        in_specs=[pl.BlockSpec((tm, ka), lambda i, j: (i, 0)),
                  pl.BlockSpec((ka, tn), lambda i, j: (0, j)),
                  pl.BlockSpec((tm, kr), lambda i, j: (i, 0)),
                  pl.BlockSpec((kr, tn), lambda i, j: (0, j)),
                  pl.BlockSpec((tm, tn), lambda i, j: (i, j + ga0)),
                  pl.BlockSpec((tm, tn), lambda i, j: (i, j + gr0))],
        out_specs=pl.BlockSpec((tm, tn), lambda i, j: (i, j)),
        out_shape=jax.ShapeDtypeStruct((m, d), BF16),
        compiler_params=_compiler_params(("parallel", "arbitrary"), vmem),
        name="merge_branches",
    )(att, w_att_o, rnn, w_rnn_o, g2, g2)


def _glu_body(a_ref, wg_ref, wu_ref, o_ref):
    a = a_ref[...]
    g = jnp.dot(a, wg_ref[...], preferred_element_type=F32)
    u = jnp.dot(a, wu_ref[...], preferred_element_type=F32)
    o_ref[...] = (g * jax.nn.sigmoid(g) * u).astype(o_ref.dtype)


def _glu_comb_body(a_ref, wg_ref, wu_ref, comb_ref, o_ref):
    a = a_ref[...]
    g = jnp.dot(a, wg_ref[...], preferred_element_type=F32)
    u = jnp.dot(a, wu_ref[...], preferred_element_type=F32)
    comb = pltpu.repeat(comb_ref[...], o_ref.shape[1] // LANES, 1)
    o_ref[...] = (g * jax.nn.sigmoid(g) * u * comb).astype(o_ref.dtype)


def _glu_vmem(tm, k, tn):
    return 2 * (_nbytes((tm, k), BF16) + 2 * _nbytes((k, tn), BF16) + _nbytes((tm, tn), BF16)
                + _nbytes((tm, LANES), F32)) + 3 * _nbytes((tm, tn), F32)


def _dense_glu(xb, w_in, *, tm=1024, tn=512):
    m, k = xb.shape
    f = w_in.shape[1] // 2
    tm, tn = _tile(m, tm), _tile(f, tn)
    nj = f // tn
    return pl.pallas_call(
        _glu_body,
        grid=(m // tm, nj),
        in_specs=[pl.BlockSpec((tm, k), lambda i, j: (i, 0)),
                  pl.BlockSpec((k, tn), lambda i, j: (0, j)),
                  pl.BlockSpec((k, tn), lambda i, j: (0, j + nj))],
        out_specs=pl.BlockSpec((tm, tn), lambda i, j: (i, j)),
        out_shape=jax.ShapeDtypeStruct((m, f), BF16),
        compiler_params=_compiler_params(("parallel", "arbitrary"), _glu_vmem(tm, k, tn)),
        name="dense_glu",
    )(xb, w_in, w_in)


def _expert_glu(xb, w_exp_in, comb_rep, *, tm=1024, tn=512):
    m, k = xb.shape
    n_exp, _, f2 = w_exp_in.shape
    f = f2 // 2
    tm, tn = _tile(m, tm), _tile(f, tn)
    r = f // tn
    return pl.pallas_call(
        _glu_comb_body,
        grid=(m // tm, n_exp * r),
        in_specs=[pl.BlockSpec((tm, k), lambda i, j: (i, 0)),
                  pl.BlockSpec((None, k, tn), lambda i, j: (j // r, 0, j % r)),
                  pl.BlockSpec((None, k, tn), lambda i, j: (j // r, 0, j % r + r)),
                  pl.BlockSpec((tm, LANES), lambda i, j: (i, j // r))],
        out_specs=pl.BlockSpec((tm, tn), lambda i, j: (i, j)),
        out_shape=jax.ShapeDtypeStruct((m, n_exp * f), BF16),
        compiler_params=_compiler_params(("parallel", "arbitrary"), _glu_vmem(tm, k, tn)),
        name="expert_glu",
    )(xb, w_exp_in, w_exp_in, comb_rep)


def _router_body(x_ref, w_ref, o_ref, *, n_exp):
    logits = jnp.dot(x_ref[...], w_ref[...], preferred_element_type=F32, precision=lax.Precision.HIGHEST)
    lane = lax.broadcasted_iota(jnp.int32, logits.shape, 1)
    logits = jnp.where(lane < n_exp, logits, NEG_INF)

    def take_top(vals):
        top = jnp.max(vals, axis=1, keepdims=True)
        idx = jnp.min(jnp.where(vals == top, lane, LANES), axis=1, keepdims=True)
        return top, idx

    v1, i1 = take_top(logits)
    v2, i2 = take_top(jnp.where(lane == i1, NEG_INF, logits))
    e2 = jnp.exp(v2 - v1)
    denom = 1.0 + e2
    w1, w2 = 1.0 / denom, e2 / denom
    for e in range(n_exp):
        comb_e = jnp.where(i1 == e, w1, 0.0) + jnp.where(i2 == e, w2, 0.0)
        o_ref[:, e * LANES:(e + 1) * LANES] = jnp.broadcast_to(comb_e, (comb_e.shape[0], LANES))


def _router(x, w_router, *, tm=512):
    m, d = x.shape
    n_exp = w_router.shape[1]
    tm = _tile(m, tm)
    w_pad = jnp.pad(w_router, ((0, 0), (0, LANES - n_exp)))
    vmem = 2 * (_nbytes((tm, d), F32) + _nbytes((d, LANES), F32) + _nbytes((tm, n_exp * LANES), F32))
    vmem += 8 * _nbytes((tm, LANES), F32) + 3 * _nbytes((tm, d), F32)
    return pl.pallas_call(
        functools.partial(_router_body, n_exp=n_exp),
        grid=(m // tm,),
        in_specs=[pl.BlockSpec((tm, d), lambda i: (i, 0)),
                  pl.BlockSpec((d, LANES), lambda i: (0, 0))],
        out_specs=pl.BlockSpec((tm, n_exp * LANES), lambda i: (i, 0)),
        out_shape=jax.ShapeDtypeStruct((m, n_exp * LANES), F32),
        compiler_params=_compiler_params(("parallel",), vmem),
        name="router_top2",
    )(x, w_pad)


def kernel(x, w_in, b_f, conv_w, conv_b, w_a, b_a, w_x, b_x, lam, w_att_o, w_rnn_o, w_out,
           ln1_g, ln1_b, w_ffn_in, w_ffn_out, w_router, w_exp_in, w_exp_out, ln2_g, ln2_b):
    batch, seq, d_model = x.shape
    depth = w_in.shape[0]
    n_heads = b_f.shape[1]
    d_att = w_att_o.shape[1]
    d_rnn = w_rnn_o.shape[1]
    assert d_att == n_heads * LANES and w_a.shape[-1] == LANES
    assert w_in.shape[2] == 3 * d_att + n_heads + 2 * d_rnn + 2 * d_model
    alpha = float((2 * depth) ** 0.25)
    attn_blk = _tile(seq, 512)
    q_scale = float(LANES ** -0.5) * LOG2_E

    f_col0 = 3 * d_att
    g2_col0 = f_col0 + n_heads
    w_qkv = w_in[:, :, :f_col0].astype(BF16)
    w_f = w_in[:, :, f_col0:g2_col0].astype(BF16)
    w_g2 = w_in[:, :, g2_col0:].astype(BF16)
    w_a, w_x = w_a.astype(BF16), w_x.astype(BF16)
    w_att_o, w_rnn_o, w_out = w_att_o.astype(BF16), w_rnn_o.astype(BF16), w_out.astype(BF16)
    w_ffn_in, w_ffn_out = w_ffn_in.astype(BF16), w_ffn_out.astype(BF16)
    w_exp_in = w_exp_in.astype(BF16)
    w_exp_out = w_exp_out.astype(BF16).reshape(w_exp_out.shape[0], -1, d_model)

    outs = []
    for bi in range(batch):
        xf = x[bi]
        xb = xf.astype(BF16)
        for l in range(depth):
            qkv = _matmul(xb, w_qkv[l], BF16, scaled_cols=d_att, col_scale=q_scale, name="proj_qkv")
            g2 = _matmul(xb, w_g2[l], F32, name="proj_rnn_gates")
            c = _decay_cumsum(xb, w_f[l], b_f[l], ts=attn_blk)
            c = c.reshape(n_heads, seq // attn_blk, attn_blk)
            att = _forgetting_attention(qkv, c, n_heads, blk=attn_blk)
            rnn = _rglru_branch(g2, d_rnn, conv_w[l], conv_b[l], w_a[l], b_a[l], w_x[l], b_x[l], lam[l])
            merged = _merge_branches(att, w_att_o[l], rnn, w_rnn_o[l], g2, 2 * d_rnn)
            mix = _matmul(merged, w_out[l], F32, name="proj_out")
            xf, xb = _deepnorm_ln(xf, mix, ln1_g[l], ln1_b[l], alpha)
            j = l // 2
            if l % 2 == 0:
                act = _dense_glu(xb, w_ffn_in[j])
                ff = _matmul(act, w_ffn_out[j], F32, tm=512, name="ffn_out")
            else:
                comb = _router(xf, w_router[j])
                act = _expert_glu(xb, w_exp_in[j], comb)
                ff = _matmul(act, w_exp_out[j], F32, tm=512, name="expert_out")
            xf, xb = _deepnorm_ln(xf, ff, ln2_g[l], ln2_b[l], alpha)
        outs.append(xf)
    return jnp.stack(outs)
```

```python
import functools

import numpy as np
import jax
import jax.numpy as jnp
from jax import lax
from jax.experimental import pallas as pl
from jax.experimental.pallas import tpu as pltpu

F32 = jnp.float32
BF16 = jnp.bfloat16

LANES = 128
SUBLANES = 8
V7X_VMEM_BYTES = 64 * 2 ** 20
VMEM_RESERVE_BYTES = 8 * 2 ** 20

CONV_WIDTH = 4
RG_LRU_C = 8.0
TOP_K = 2
LN_EPS = 1e-5
NEG_INF = float("-inf")
LOG2_E = float(np.log2(np.e))
HEADS_PER_STEP = 2


def _compiler_params(semantics, block_bytes):
    limit = min(int(block_bytes) + VMEM_RESERVE_BYTES, V7X_VMEM_BYTES - VMEM_RESERVE_BYTES)
    return pltpu.CompilerParams(dimension_semantics=semantics, vmem_limit_bytes=limit)


def _nbytes(shape, dtype):
    return int(np.prod(shape)) * jnp.dtype(dtype).itemsize


def _tile(dim, pref):
    if dim <= pref:
        return dim
    t = pref - pref % LANES
    while dim % t:
        t -= LANES
    assert t > 0, (dim, pref)
    return t


def _mm_body(a_ref, b_ref, o_ref, *, scaled_cols, col_scale):
    acc = jnp.dot(a_ref[...], b_ref[...], preferred_element_type=F32)
    if scaled_cols:
        acc = acc * jnp.where(pl.program_id(1) * o_ref.shape[1] < scaled_cols, col_scale, 1.0)
    o_ref[...] = acc.astype(o_ref.dtype)


def _matmul(a, b, layer, out_dtype, *, tm=1024, tn=512, col0=0, n=None, scaled_cols=0, col_scale=1.0, name):
    m, k = a.shape
    n = b.shape[2] - col0 if n is None else n
    tm, tn = _tile(m, tm), _tile(n, tn)
    assert scaled_cols % tn == 0 and col0 % tn == 0
    jb0 = col0 // tn
    vmem = 2 * (_nbytes((tm, k), a.dtype) + _nbytes((k, tn), b.dtype) + _nbytes((tm, tn), out_dtype))
    vmem += _nbytes((tm, tn), F32)
    return pl.pallas_call(
        functools.partial(_mm_body, scaled_cols=scaled_cols, col_scale=col_scale),
        grid=(m // tm, n // tn),
        in_specs=[pl.BlockSpec((tm, k), lambda i, j: (i, 0)),
                  pl.BlockSpec((None, k, tn), lambda i, j: (layer, 0, j + jb0))],
        out_specs=pl.BlockSpec((tm, tn), lambda i, j: (i, j)),
        out_shape=jax.ShapeDtypeStruct((m, n), out_dtype),
        compiler_params=_compiler_params(("parallel", "arbitrary"), vmem),
        name=name,
    )(a, b)


def _ln_body(x_ref, y_ref, g_ref, b_ref, o_ref, ob_ref, *, alpha):
    z = alpha * x_ref[...] + y_ref[...]
    mu = jnp.mean(z, axis=-1, keepdims=True)
    zc = z - mu
    var = jnp.mean(zc * zc, axis=-1, keepdims=True)
    y = zc * lax.rsqrt(var + LN_EPS)
    o = y * g_ref[...] + b_ref[...]
    o_ref[...] = o
    ob_ref[...] = o.astype(BF16)


def _deepnorm_ln(x, y, g, b, alpha, *, tm=256):
    m, d = x.shape
    tm = _tile(m, tm)
    row = pl.BlockSpec((tm, d), lambda i: (i, 0))
    vec = pl.BlockSpec((1, d), lambda i: (0, 0))
    vmem = 2 * (3 * _nbytes((tm, d), F32) + _nbytes((tm, d), BF16)) + 3 * _nbytes((tm, d), F32)
    return pl.pallas_call(
        functools.partial(_ln_body, alpha=alpha),
        grid=(m // tm,),
        in_specs=[row, row, vec, vec],
        out_specs=[row, row],
        out_shape=[jax.ShapeDtypeStruct((m, d), F32), jax.ShapeDtypeStruct((m, d), BF16)],
        compiler_params=_compiler_params(("parallel",), vmem),
        name="deepnorm_ln",
    )(x, y, g.reshape(1, d), b.reshape(1, d))


def _shift_rows(x, d, fill):
    rows = lax.broadcasted_iota(jnp.int32, x.shape, 0)
    return jnp.where(rows >= d, pltpu.roll(x, d, 0), fill)


def _linear_scan_rows(a, b):
    d = 1
    while d < a.shape[0]:
        b = a * _shift_rows(b, d, 0.0) + b
        a = a * _shift_rows(a, d, 1.0)
        d *= 2
    return a, b


def _cumsum_rows(x):
    d = 1
    while d < x.shape[0]:
        x = x + _shift_rows(x, d, 0.0)
        d *= 2
    return x


def _sigmoid(z):
    return 1.0 / (1.0 + jnp.exp(-z))


def _log_sigmoid(z):
    return jnp.minimum(z, 0.0) - jnp.log1p(jnp.exp(-jnp.abs(z)))


def _softplus(z):
    return jnp.maximum(z, 0.0) + jnp.log1p(jnp.exp(-jnp.abs(z)))


def _decay_body(x_ref, w_ref, bf_ref, o_ref, carry_ref):
    @pl.when(pl.program_id(0) == 0)
    def _():
        carry_ref[...] = jnp.zeros_like(carry_ref)

    z = jnp.dot(x_ref[...], w_ref[...], preferred_element_type=F32) + bf_ref[...]
    c = _cumsum_rows(_log_sigmoid(z)) + carry_ref[0:1, :]
    t = c.shape[0]
    carry_ref[0:1, :] = c[t - 1:t, :]
    o_ref[...] = (c * LOG2_E).T[:o_ref.shape[0], :]


def _decay_cumsum(xb, w_f, b_f, *, ts):
    s, d = xb.shape
    hp = max(SUBLANES, -(-b_f.shape[0] // SUBLANES) * SUBLANES)
    pad = LANES - b_f.shape[0]
    w_pad = jnp.pad(w_f, ((0, 0), (0, pad)))
    b_pad = jnp.pad(b_f, (0, pad)).reshape(1, LANES)
    vmem = 2 * (_nbytes((ts, d), BF16) + _nbytes((d, LANES), BF16) + _nbytes((hp, ts), F32))
    vmem += 6 * _nbytes((ts, LANES), F32)
    out = pl.pallas_call(
        _decay_body,
        grid=(s // ts,),
        in_specs=[pl.BlockSpec((ts, d), lambda t: (t, 0)),
                  pl.BlockSpec((d, LANES), lambda t: (0, 0)),
                  pl.BlockSpec((1, LANES), lambda t: (0, 0))],
        out_specs=pl.BlockSpec((hp, ts), lambda t: (0, t)),
        out_shape=jax.ShapeDtypeStruct((hp, s), F32),
        scratch_shapes=[pltpu.VMEM((SUBLANES, LANES), F32)],
        compiler_params=_compiler_params(("arbitrary",), vmem),
        name="decay_cumsum",
    )(xb, w_pad, b_pad)
    return out[:b_f.shape[0]]


def _attn_body(q_ref, k_ref, v_ref, c_ref, o_ref, m_ref, l_ref, acc_ref, p_ref, corr_ref, *, blk):
    i = pl.program_id(1)
    dh = LANES
    n_sub = q_ref.shape[1] // dh
    reps = blk // LANES
    ones_col = (lax.broadcasted_iota(jnp.int32, (blk, dh), 1) == 0).astype(BF16)

    def key_rows(j):
        return pl.ds(pl.multiple_of(j * blk, blk), blk)

    def scores(j, slot, masked=False):
        for a in range(n_sub):
            cols = slice(a * dh, (a + 1) * dh)
            s = lax.dot_general(q_ref[:, cols], k_ref[key_rows(j), cols], (((1,), (1,)), ((), ())),
                                preferred_element_type=F32)
            s = s - c_ref[a, pl.ds(j, 1), :]
            if masked:
                qpos = lax.broadcasted_iota(jnp.int32, s.shape, 0)
                kpos = lax.broadcasted_iota(jnp.int32, s.shape, 1)
                s = jnp.where(kpos <= qpos, s, NEG_INF)
            m_prev = m_ref[a]
            m_new = jnp.maximum(m_prev, jnp.max(s, axis=1, keepdims=True))
            p_ref[slot, a] = jnp.exp2(s - jnp.tile(m_new, (1, reps))).astype(BF16)
            corr_ref[slot, a] = jnp.exp2(m_prev - m_new)
            m_ref[a] = m_new

    def values(j, slot):
        for a in range(n_sub):
            v = v_ref[key_rows(j), a * dh:(a + 1) * dh]
            pv = jnp.dot(p_ref[slot, a], jnp.concatenate([v, ones_col], axis=1), preferred_element_type=F32)
            corr = corr_ref[slot, a]
            acc_ref[a] = corr * acc_ref[a] + pv[:, :dh]
            l_ref[a] = corr * l_ref[a] + pv[:, dh:]

    m_ref[...] = jnp.full_like(m_ref, NEG_INF)
    l_ref[...] = jnp.zeros_like(l_ref)
    acc_ref[...] = jnp.zeros_like(acc_ref)
    scores(i, 0, masked=True)
    n_pairs = i // 2

    def pair(u, carry):
        scores(2 * u, 1)
        values(jnp.where(u == 0, i, 2 * u - 1), 0)
        scores(2 * u + 1, 0)
        values(2 * u, 1)
        return carry

    lax.fori_loop(0, n_pairs, pair, 0)
    parked = jnp.where(n_pairs == 0, i, 2 * n_pairs - 1)

    @pl.when(i % 2 == 1)
    def _():
        scores(i - 1, 1)
        values(parked, 0)
        values(i - 1, 1)

    @pl.when(i % 2 == 0)
    def _():
        values(parked, 0)

    for a in range(n_sub):
        l = jnp.sum(l_ref[a], axis=1, keepdims=True)
        o_ref[:, a * dh:(a + 1) * dh] = (acc_ref[a] / l).astype(o_ref.dtype)


def _forgetting_attention(qkv, c, n_heads, *, blk):
    s = qkv.shape[0]
    dh = LANES
    nb = s // blk
    hps = HEADS_PER_STEP if n_heads % HEADS_PER_STEP == 0 else 1
    w = hps * dh
    groups = n_heads // hps
    head_cols = lambda off: pl.BlockSpec((s, w), lambda h, i: (0, off + h))
    vmem = 2 * (2 * _nbytes((s, w), BF16) + 2 * _nbytes((blk, w), BF16) + _nbytes((hps, nb, blk), F32))
    acc_like = pltpu.VMEM((hps, blk, dh), F32)
    p_slots = pltpu.VMEM((2, hps, blk, blk), BF16)
    corr_slots = pltpu.VMEM((2, hps, blk, dh), F32)
    vmem += 5 * _nbytes((hps, blk, dh), F32) + _nbytes((2, hps, blk, blk), BF16)
    vmem += 4 * hps * _nbytes((blk, blk), F32)
    return pl.pallas_call(
        functools.partial(_attn_body, blk=blk),
        grid=(groups, nb),
        in_specs=[pl.BlockSpec((blk, w), lambda h, i: (i, h)),
                  head_cols(groups), head_cols(2 * groups),
                  pl.BlockSpec((hps, nb, blk), lambda h, i: (h, 0, 0))],
        out_specs=pl.BlockSpec((blk, w), lambda h, i: (i, h)),
        out_shape=jax.ShapeDtypeStruct((s, n_heads * dh), BF16),
        scratch_shapes=[acc_like, acc_like, acc_like, p_slots, corr_slots],
        compiler_params=_compiler_params(("parallel", "arbitrary"), vmem),
        name="forgetting_attention",
    )(qkv, qkv, qkv, c)


def _gelu_tanh(x):
    z = np.float32(np.sqrt(2.0 / np.pi)) * (x + 0.044715 * (x * x * x))
    return x * _sigmoid(2.0 * z)


def _rglru_body(rx_ref, ry_ref, cw_ref, cb_ref, wa_ref, ba_ref, wx_ref, bx_ref, lam_ref,
                o_ref, xbuf_ref, h_ref):
    ts, tc = rx_ref.shape
    halo = SUBLANES

    @pl.when(pl.program_id(1) == 0)
    def _():
        xbuf_ref[0:halo, :] = jnp.zeros((halo, tc), F32)
        h_ref[...] = jnp.zeros_like(h_ref)

    xbuf_ref[halo:halo + ts, :] = rx_ref[...]
    xc = cb_ref[...]
    for j in range(CONV_WIDTH):
        off = halo - (CONV_WIDTH - 1) + j
        xc = xc + xbuf_ref[off:off + ts, :] * cw_ref[j:j + 1, :]
    xbuf_ref[0:halo, :] = xbuf_ref[ts:ts + halo, :]

    xcb = xc.astype(BF16)
    nblk = wa_ref.shape[0]
    rb = tc // nblk
    gate = lambda w_ref: jnp.concatenate(
        [jnp.dot(xcb[:, n * rb:(n + 1) * rb], w_ref[n], preferred_element_type=F32) for n in range(nblk)],
        axis=1)
    r = _sigmoid(gate(wa_ref) + ba_ref[...])
    i = _sigmoid(gate(wx_ref) + bx_ref[...])
    log_a = -RG_LRU_C * r * _softplus(-lam_ref[...])
    a = jnp.exp(log_a)
    u = (i * xc) * jnp.sqrt(-jnp.tanh(log_a) * (a * a + 1.0))
    a_cum, h = _linear_scan_rows(a, u)
    h = h + a_cum * h_ref[0:1, :]
    h_ref[0:1, :] = h[ts - 1:ts, :]
    o_ref[...] = (_gelu_tanh(ry_ref[...]) * h).astype(o_ref.dtype)


def _rglru_branch(g2, d_rnn, conv_w, conv_b, w_a, b_a, w_x, b_x, lam, *, ts=256, tc=512):
    s = g2.shape[0]
    ts, tc = _tile(s, ts), _tile(d_rnn, tc)
    rb = w_a.shape[-1]
    nblk = tc // rb
    ncb = d_rnn // tc
    vec = lambda v: v.reshape(1, d_rnn)
    vspec = pl.BlockSpec((1, tc), lambda c, t: (0, c))
    wspec = pl.BlockSpec((nblk, rb, rb), lambda c, t: (c, 0, 0))
    vmem = 2 * (2 * _nbytes((ts, tc), F32) + _nbytes((ts, tc), BF16) + 2 * _nbytes((nblk, rb, rb), BF16))
    vmem += 14 * _nbytes((ts, tc), F32)
    return pl.pallas_call(
        _rglru_body,
        grid=(ncb, s // ts),
        in_specs=[pl.BlockSpec((ts, tc), lambda c, t: (t, c)),
                  pl.BlockSpec((ts, tc), lambda c, t: (t, c + ncb)),
                  pl.BlockSpec((CONV_WIDTH, tc), lambda c, t: (0, c)), vspec,
                  wspec, vspec, wspec, vspec, vspec],
        out_specs=pl.BlockSpec((ts, tc), lambda c, t: (t, c)),
        out_shape=jax.ShapeDtypeStruct((s, d_rnn), BF16),
        scratch_shapes=[pltpu.VMEM((ts + SUBLANES, tc), F32), pltpu.VMEM((SUBLANES, tc), F32)],
        compiler_params=_compiler_params(("parallel", "arbitrary"), vmem),
        name="rglru_branch",
    )(g2, g2, conv_w, vec(conv_b), w_a, vec(b_a), w_x, vec(b_x), vec(lam))


def _merge_body(att_ref, wo_a_ref, rnn_ref, wo_r_ref, ga_ref, gr_ref, o_ref):
    att = jnp.dot(att_ref[...], wo_a_ref[...], preferred_element_type=F32)
    rnn = jnp.dot(rnn_ref[...], wo_r_ref[...], preferred_element_type=F32)
    o_ref[...] = (_sigmoid(ga_ref[...]) * att + _sigmoid(gr_ref[...]) * rnn).astype(o_ref.dtype)


def _merge_branches(att, w_att_o, rnn, w_rnn_o, layer, g2, gate_col0, *, tm=1024, tn=512):
    m, ka = att.shape
    kr = rnn.shape[1]
    d = w_att_o.shape[2]
    tm, tn = _tile(m, tm), _tile(d, tn)
    ga0, gr0 = gate_col0 // tn, (gate_col0 + d) // tn
    vmem = 2 * (_nbytes((tm, ka + kr), BF16) + _nbytes((ka + kr, tn), BF16) + 2 * _nbytes((tm, tn), F32)
                + _nbytes((tm, tn), BF16)) + 3 * _nbytes((tm, tn), F32)
    return pl.pallas_call(
        _merge_body,
        grid=(m // tm, d // tn),
        in_specs=[pl.BlockSpec((tm, ka), lambda i, j: (i, 0)),
                  pl.BlockSpec((None, ka, tn), lambda i, j: (layer, 0, j)),
                  pl.BlockSpec((tm, kr), lambda i, j: (i, 0)),
                  pl.BlockSpec((None, kr, tn), lambda i, j: (layer, 0, j)),
                  pl.BlockSpec((tm, tn), lambda i, j: (i, j + ga0)),
                  pl.BlockSpec((tm, tn), lambda i, j: (i, j + gr0))],
        out_specs=pl.BlockSpec((tm, tn), lambda i, j: (i, j)),
        out_shape=jax.ShapeDtypeStruct((m, d), BF16),
        compiler_params=_compiler_params(("parallel", "arbitrary"), vmem),
        name="merge_branches",
    )(att, w_att_o, rnn, w_rnn_o, g2, g2)


def _glu_body(a_ref, wg_ref, wu_ref, o_ref):
    a = a_ref[...]
    g = jnp.dot(a, wg_ref[...], preferred_element_type=F32)
    u = jnp.dot(a, wu_ref[...], preferred_element_type=F32)
    o_ref[...] = (g * _sigmoid(g) * u).astype(o_ref.dtype)


def _glu_comb_body(a_ref, wg_ref, wu_ref, comb_ref, o_ref):
    a = a_ref[...]
    g = jnp.dot(a, wg_ref[...], preferred_element_type=F32)
    u = jnp.dot(a, wu_ref[...], preferred_element_type=F32)
    comb = jnp.tile(comb_ref[...], (1, o_ref.shape[1] // LANES))
    o_ref[...] = (g * _sigmoid(g) * u * comb).astype(o_ref.dtype)


def _glu_vmem(tm, k, tn):
    return 2 * (_nbytes((tm, k), BF16) + 2 * _nbytes((k, tn), BF16) + _nbytes((tm, tn), BF16)
                + _nbytes((tm, LANES), F32)) + 3 * _nbytes((tm, tn), F32)


def _dense_glu(xb, w_in, layer, *, tm=1024, tn=512):
    m, k = xb.shape
    f = w_in.shape[2] // 2
    tm, tn = _tile(m, tm), _tile(f, tn)
    nj = f // tn
    return pl.pallas_call(
        _glu_body,
        grid=(m // tm, nj),
        in_specs=[pl.BlockSpec((tm, k), lambda i, j: (i, 0)),
                  pl.BlockSpec((None, k, tn), lambda i, j: (layer, 0, j)),
                  pl.BlockSpec((None, k, tn), lambda i, j: (layer, 0, j + nj))],
        out_specs=pl.BlockSpec((tm, tn), lambda i, j: (i, j)),
        out_shape=jax.ShapeDtypeStruct((m, f), BF16),
        compiler_params=_compiler_params(("parallel", "arbitrary"), _glu_vmem(tm, k, tn)),
        name="dense_glu",
    )(xb, w_in, w_in)


def _expert_glu(xb, w_exp_in, layer, comb_rep, *, tm=1024, tn=512):
    m, k = xb.shape
    _, n_exp, _, f2 = w_exp_in.shape
    f = f2 // 2
    tm, tn = _tile(m, tm), _tile(f, tn)
    r = f // tn
    return pl.pallas_call(
        _glu_comb_body,
        grid=(m // tm, n_exp * r),
        in_specs=[pl.BlockSpec((tm, k), lambda i, j: (i, 0)),
                  pl.BlockSpec((None, None, k, tn), lambda i, j: (layer, j // r, 0, j % r)),
                  pl.BlockSpec((None, None, k, tn), lambda i, j: (layer, j // r, 0, j % r + r)),
                  pl.BlockSpec((tm, LANES), lambda i, j: (i, j // r))],
        out_specs=pl.BlockSpec((tm, tn), lambda i, j: (i, j)),
        out_shape=jax.ShapeDtypeStruct((m, n_exp * f), BF16),
        compiler_params=_compiler_params(("parallel", "arbitrary"), _glu_vmem(tm, k, tn)),
        name="expert_glu",
    )(xb, w_exp_in, w_exp_in, comb_rep)


def _router_body(x_ref, w_ref, o_ref, *, n_exp):
    logits = jnp.dot(x_ref[...], w_ref[...], preferred_element_type=F32, precision=lax.Precision.HIGHEST)
    lane = lax.broadcasted_iota(jnp.int32, logits.shape, 1)
    logits = jnp.where(lane < n_exp, logits, NEG_INF)

    def take_top(vals):
        top = jnp.max(vals, axis=1, keepdims=True)
        idx = jnp.min(jnp.where(vals == top, lane, LANES), axis=1, keepdims=True)
        return top, idx

    v1, i1 = take_top(logits)
    v2, i2 = take_top(jnp.where(lane == i1, NEG_INF, logits))
    e2 = jnp.exp(v2 - v1)
    denom = 1.0 + e2
    w1, w2 = 1.0 / denom, e2 / denom
    for e in range(n_exp):
        comb_e = jnp.where(i1 == e, w1, 0.0) + jnp.where(i2 == e, w2, 0.0)
        o_ref[:, e * LANES:(e + 1) * LANES] = jnp.broadcast_to(comb_e, (comb_e.shape[0], LANES))


def _router(x, w_router, *, tm=512):
    m, d = x.shape
    n_exp = w_router.shape[1]
    tm = _tile(m, tm)
    w_pad = jnp.pad(w_router, ((0, 0), (0, LANES - n_exp)))
    vmem = 2 * (_nbytes((tm, d), F32) + _nbytes((d, LANES), F32) + _nbytes((tm, n_exp * LANES), F32))
    vmem += 8 * _nbytes((tm, LANES), F32) + 3 * _nbytes((tm, d), F32)
    return pl.pallas_call(
        functools.partial(_router_body, n_exp=n_exp),
        grid=(m // tm,),
        in_specs=[pl.BlockSpec((tm, d), lambda i: (i, 0)),
                  pl.BlockSpec((d, LANES), lambda i: (0, 0))],
        out_specs=pl.BlockSpec((tm, n_exp * LANES), lambda i: (i, 0)),
        out_shape=jax.ShapeDtypeStruct((m, n_exp * LANES), F32),
        compiler_params=_compiler_params(("parallel",), vmem),
        name="router_top2",
    )(x, w_pad)


def kernel(x, w_in, b_f, conv_w, conv_b, w_a, b_a, w_x, b_x, lam, w_att_o, w_rnn_o, w_out,
           ln1_g, ln1_b, w_ffn_in, w_ffn_out, w_router, w_exp_in, w_exp_out, ln2_g, ln2_b):
    batch, seq, d_model = x.shape
    depth = w_in.shape[0]
    n_heads = b_f.shape[1]
    d_att = w_att_o.shape[1]
    d_rnn = w_rnn_o.shape[1]
    assert d_att == n_heads * LANES and w_a.shape[-1] == LANES
    assert w_in.shape[2] == 3 * d_att + n_heads + 2 * d_rnn + 2 * d_model
    alpha = float((2 * depth) ** 0.25)
    attn_blk = _tile(seq, 512)
    q_scale = float(LANES ** -0.5) * LOG2_E

    f_col0 = 3 * d_att
    g2_col0 = f_col0 + n_heads
    w_in_b = w_in.astype(BF16)
    w_f = w_in_b[:, :, f_col0:g2_col0]
    w_g2 = w_in_b[:, :, g2_col0:]
    w_a, w_x = w_a.astype(BF16), w_x.astype(BF16)
    w_att_o, w_rnn_o, w_out = w_att_o.astype(BF16), w_rnn_o.astype(BF16), w_out.astype(BF16)
    w_ffn_in, w_ffn_out = w_ffn_in.astype(BF16), w_ffn_out.astype(BF16)
    w_exp_in = w_exp_in.astype(BF16)
    w_exp_out = w_exp_out.astype(BF16).reshape(w_exp_out.shape[0], -1, d_model)

    outs = []
    for bi in range(batch):
        xf = x[bi]
        xb = xf.astype(BF16)
        for l in range(depth):
            qkv = _matmul(xb, w_in_b, l, BF16, n=f_col0, scaled_cols=d_att, col_scale=q_scale, name="proj_qkv")
            g2 = _matmul(xb, w_g2, l, F32, name="proj_rnn_gates")
            c = _decay_cumsum(xb, w_f[l], b_f[l], ts=attn_blk)
            c = c.reshape(n_heads, seq // attn_blk, attn_blk)
            att = _forgetting_attention(qkv, c, n_heads, blk=attn_blk)
            rnn = _rglru_branch(g2, d_rnn, conv_w[l], conv_b[l], w_a[l], b_a[l], w_x[l], b_x[l], lam[l])
            merged = _merge_branches(att, w_att_o, rnn, w_rnn_o, l, g2, 2 * d_rnn)
            mix = _matmul(merged, w_out, l, F32, name="proj_out")
            xf, xb = _deepnorm_ln(xf, mix, ln1_g[l], ln1_b[l], alpha)
            j = l // 2
            if l % 2 == 0:
                act = _dense_glu(xb, w_ffn_in, j)
                ff = _matmul(act, w_ffn_out, j, F32, tm=512, name="ffn_out")
            else:
                comb = _router(xf, w_router[j])
                act = _expert_glu(xb, w_exp_in, j, comb)
                ff = _matmul(act, w_exp_out, j, F32, tm=512, name="expert_out")
            xf, xb = _deepnorm_ln(xf, ff, ln2_g[l], ln2_b[l], alpha)
        outs.append(xf)
    return jnp.stack(outs)
```

```python
import functools

import numpy as np
import jax
import jax.numpy as jnp
from jax import lax
from jax.experimental import pallas as pl
from jax.experimental.pallas import tpu as pltpu

F32 = jnp.float32
BF16 = jnp.bfloat16

LANES = 128
SUBLANES = 8
V7X_VMEM_BYTES = 64 * 2 ** 20
VMEM_RESERVE_BYTES = 8 * 2 ** 20

CONV_WIDTH = 4
RG_LRU_C = 8.0
TOP_K = 2
LN_EPS = 1e-5
NEG_INF = float("-inf")
LOG2_E = float(np.log2(np.e))
HEADS_PER_STEP = 2
MOE_TILE = 512


def _compiler_params(semantics, block_bytes):
    limit = min(int(block_bytes) + VMEM_RESERVE_BYTES, V7X_VMEM_BYTES - VMEM_RESERVE_BYTES)
    return pltpu.CompilerParams(dimension_semantics=semantics, vmem_limit_bytes=limit)


def _nbytes(shape, dtype):
    return int(np.prod(shape)) * jnp.dtype(dtype).itemsize


def _tile(dim, pref):
    if dim <= pref:
        return dim
    t = pref - pref % LANES
    while dim % t:
        t -= LANES
    assert t > 0, (dim, pref)
    return t


def _mm_body(a_ref, b_ref, o_ref, *, scaled_cols, col_scale):
    acc = jnp.dot(a_ref[...], b_ref[...], preferred_element_type=F32)
    if scaled_cols:
        acc = acc * jnp.where(pl.program_id(1) * o_ref.shape[1] < scaled_cols, col_scale, 1.0)
    o_ref[...] = acc.astype(o_ref.dtype)


def _matmul(a, b, layer, out_dtype, *, tm=1024, tn=512, col0=0, n=None, scaled_cols=0, col_scale=1.0, name):
    m, k = a.shape
    n = b.shape[2] - col0 if n is None else n
    tm, tn = _tile(m, tm), _tile(n, tn)
    assert scaled_cols % tn == 0 and col0 % tn == 0
    jb0 = col0 // tn
    vmem = 2 * (_nbytes((tm, k), a.dtype) + _nbytes((k, tn), b.dtype) + _nbytes((tm, tn), out_dtype))
    vmem += _nbytes((tm, tn), F32)
    return pl.pallas_call(
        functools.partial(_mm_body, scaled_cols=scaled_cols, col_scale=col_scale),
        grid=(m // tm, n // tn),
        in_specs=[pl.BlockSpec((tm, k), lambda i, j: (i, 0)),
                  pl.BlockSpec((None, k, tn), lambda i, j: (layer, 0, j + jb0))],
        out_specs=pl.BlockSpec((tm, tn), lambda i, j: (i, j)),
        out_shape=jax.ShapeDtypeStruct((m, n), out_dtype),
        compiler_params=_compiler_params(("parallel", "arbitrary"), vmem),
        name=name,
    )(a, b)


def _ln_body(x_ref, y_ref, g_ref, b_ref, o_ref, ob_ref, *, alpha):
    z = alpha * x_ref[...] + y_ref[...]
    mu = jnp.mean(z, axis=-1, keepdims=True)
    zc = z - mu
    var = jnp.mean(zc * zc, axis=-1, keepdims=True)
    y = zc * lax.rsqrt(var + LN_EPS)
    o = y * g_ref[...] + b_ref[...]
    o_ref[...] = o
    ob_ref[...] = o.astype(BF16)


def _deepnorm_ln(x, y, g, b, alpha, *, tm=256):
    m, d = x.shape
    tm = _tile(m, tm)
    row = pl.BlockSpec((tm, d), lambda i: (i, 0))
    vec = pl.BlockSpec((1, d), lambda i: (0, 0))
    vmem = 2 * (3 * _nbytes((tm, d), F32) + _nbytes((tm, d), BF16)) + 3 * _nbytes((tm, d), F32)
    return pl.pallas_call(
        functools.partial(_ln_body, alpha=alpha),
        grid=(m // tm,),
        in_specs=[row, row, vec, vec],
        out_specs=[row, row],
        out_shape=[jax.ShapeDtypeStruct((m, d), F32), jax.ShapeDtypeStruct((m, d), BF16)],
        compiler_params=_compiler_params(("parallel",), vmem),
        name="deepnorm_ln",
    )(x, y, g.reshape(1, d), b.reshape(1, d))


def _shift_rows(x, d, fill):
    rows = lax.broadcasted_iota(jnp.int32, x.shape, 0)
    return jnp.where(rows >= d, pltpu.roll(x, d, 0), fill)


def _linear_scan_rows(a, b):
    d = 1
    while d < a.shape[0]:
        b = a * _shift_rows(b, d, 0.0) + b
        a = a * _shift_rows(a, d, 1.0)
        d *= 2
    return a, b


def _cumsum_rows(x):
    d = 1
    while d < x.shape[0]:
        x = x + _shift_rows(x, d, 0.0)
        d *= 2
    return x


def _sigmoid(z):
    return 1.0 / (1.0 + jnp.exp(-z))


def _log_sigmoid(z):
    return jnp.minimum(z, 0.0) - jnp.log1p(jnp.exp(-jnp.abs(z)))


def _softplus(z):
    return jnp.maximum(z, 0.0) + jnp.log1p(jnp.exp(-jnp.abs(z)))


def _decay_body(x_ref, w_ref, bf_ref, o_ref, carry_ref):
    @pl.when(pl.program_id(0) == 0)
    def _():
        carry_ref[...] = jnp.zeros_like(carry_ref)

    z = jnp.dot(x_ref[...], w_ref[...], preferred_element_type=F32) + bf_ref[...]
    c = _cumsum_rows(_log_sigmoid(z)) + carry_ref[0:1, :]
    t = c.shape[0]
    carry_ref[0:1, :] = c[t - 1:t, :]
    o_ref[...] = (c * LOG2_E).T[:o_ref.shape[0], :]


def _decay_cumsum(xb, w_f, b_f, *, ts):
    s, d = xb.shape
    hp = max(SUBLANES, -(-b_f.shape[0] // SUBLANES) * SUBLANES)
    pad = LANES - b_f.shape[0]
    w_pad = jnp.pad(w_f, ((0, 0), (0, pad)))
    b_pad = jnp.pad(b_f, (0, pad)).reshape(1, LANES)
    vmem = 2 * (_nbytes((ts, d), BF16) + _nbytes((d, LANES), BF16) + _nbytes((hp, ts), F32))
    vmem += 6 * _nbytes((ts, LANES), F32)
    out = pl.pallas_call(
        _decay_body,
        grid=(s // ts,),
        in_specs=[pl.BlockSpec((ts, d), lambda t: (t, 0)),
                  pl.BlockSpec((d, LANES), lambda t: (0, 0)),
                  pl.BlockSpec((1, LANES), lambda t: (0, 0))],
        out_specs=pl.BlockSpec((hp, ts), lambda t: (0, t)),
        out_shape=jax.ShapeDtypeStruct((hp, s), F32),
        scratch_shapes=[pltpu.VMEM((SUBLANES, LANES), F32)],
        compiler_params=_compiler_params(("arbitrary",), vmem),
        name="decay_cumsum",
    )(xb, w_pad, b_pad)
    return out[:b_f.shape[0]]


def _attn_body(q_ref, k_ref, v_ref, c_ref, o_ref, m_ref, l_ref, acc_ref, p_ref, corr_ref, *, blk):
    i = pl.program_id(1)
    dh = LANES
    n_sub = q_ref.shape[1] // dh
    reps = blk // LANES
    ones_col = (lax.broadcasted_iota(jnp.int32, (blk, dh), 1) == 0).astype(BF16)

    def key_rows(j):
        return pl.ds(pl.multiple_of(j * blk, blk), blk)

    def scores(j, slot, masked=False):
        for a in range(n_sub):
            cols = slice(a * dh, (a + 1) * dh)
            s = lax.dot_general(q_ref[:, cols], k_ref[key_rows(j), cols], (((1,), (1,)), ((), ())),
                                preferred_element_type=F32)
            s = s - c_ref[a, pl.ds(j, 1), :]
            if masked:
                qpos = lax.broadcasted_iota(jnp.int32, s.shape, 0)
                kpos = lax.broadcasted_iota(jnp.int32, s.shape, 1)
                s = jnp.where(kpos <= qpos, s, NEG_INF)
            m_prev = m_ref[a]
            m_new = jnp.maximum(m_prev, jnp.max(s, axis=1, keepdims=True))
            p_ref[slot, a] = jnp.exp2(s - jnp.tile(m_new, (1, reps))).astype(BF16)
            corr_ref[slot, a] = jnp.exp2(m_prev - m_new)
            m_ref[a] = m_new

    def values(j, slot):
        for a in range(n_sub):
            v = v_ref[key_rows(j), a * dh:(a + 1) * dh]
            pv = jnp.dot(p_ref[slot, a], jnp.concatenate([v, ones_col], axis=1), preferred_element_type=F32)
            corr = corr_ref[slot, a]
            acc_ref[a] = corr * acc_ref[a] + pv[:, :dh]
            l_ref[a] = corr * l_ref[a] + pv[:, dh:]

    m_ref[...] = jnp.full_like(m_ref, NEG_INF)
    l_ref[...] = jnp.zeros_like(l_ref)
    acc_ref[...] = jnp.zeros_like(acc_ref)
    scores(i, 0, masked=True)
    n_pairs = i // 2

    def pair(u, carry):
        scores(2 * u, 1)
        values(jnp.where(u == 0, i, 2 * u - 1), 0)
        scores(2 * u + 1, 0)
        values(2 * u, 1)
        return carry

    lax.fori_loop(0, n_pairs, pair, 0)
    parked = jnp.where(n_pairs == 0, i, 2 * n_pairs - 1)

    @pl.when(i % 2 == 1)
    def _():
        scores(i - 1, 1)
        values(parked, 0)
        values(i - 1, 1)

    @pl.when(i % 2 == 0)
    def _():
        values(parked, 0)

    for a in range(n_sub):
        l = jnp.sum(l_ref[a], axis=1, keepdims=True)
        o_ref[:, a * dh:(a + 1) * dh] = (acc_ref[a] / l).astype(o_ref.dtype)


def _forgetting_attention(qkv, c, n_heads, *, blk):
    s = qkv.shape[0]
    dh = LANES
    nb = s // blk
    hps = HEADS_PER_STEP if n_heads % HEADS_PER_STEP == 0 else 1
    w = hps * dh
    groups = n_heads // hps
    head_cols = lambda off: pl.BlockSpec((s, w), lambda h, i: (0, off + h))
    vmem = 2 * (2 * _nbytes((s, w), BF16) + 2 * _nbytes((blk, w), BF16) + _nbytes((hps, nb, blk), F32))
    acc_like = pltpu.VMEM((hps, blk, dh), F32)
    p_slots = pltpu.VMEM((2, hps, blk, blk), BF16)
    corr_slots = pltpu.VMEM((2, hps, blk, dh), F32)
    vmem += 5 * _nbytes((hps, blk, dh), F32) + _nbytes((2, hps, blk, blk), BF16)
    vmem += 4 * hps * _nbytes((blk, blk), F32)
    return pl.pallas_call(
        functools.partial(_attn_body, blk=blk),
        grid=(groups, nb),
        in_specs=[pl.BlockSpec((blk, w), lambda h, i: (i, h)),
                  head_cols(groups), head_cols(2 * groups),
                  pl.BlockSpec((hps, nb, blk), lambda h, i: (h, 0, 0))],
        out_specs=pl.BlockSpec((blk, w), lambda h, i: (i, h)),
        out_shape=jax.ShapeDtypeStruct((s, n_heads * dh), BF16),
        scratch_shapes=[acc_like, acc_like, acc_like, p_slots, corr_slots],
        compiler_params=_compiler_params(("parallel", "arbitrary"), vmem),
        name="forgetting_attention",
    )(qkv, qkv, qkv, c)


def _gelu_tanh(x):
    z = np.float32(np.sqrt(2.0 / np.pi)) * (x + 0.044715 * (x * x * x))
    return x * _sigmoid(2.0 * z)


def _rglru_body(rx_ref, ry_ref, cw_ref, cb_ref, wa_ref, ba_ref, wx_ref, bx_ref, lam_ref,
                o_ref, xbuf_ref, h_ref):
    ts, tc = rx_ref.shape
    halo = SUBLANES

    @pl.when(pl.program_id(1) == 0)
    def _():
        xbuf_ref[0:halo, :] = jnp.zeros((halo, tc), F32)
        h_ref[...] = jnp.zeros_like(h_ref)

    xbuf_ref[halo:halo + ts, :] = rx_ref[...]
    xc = cb_ref[...]
    for j in range(CONV_WIDTH):
        off = halo - (CONV_WIDTH - 1) + j
        xc = xc + xbuf_ref[off:off + ts, :] * cw_ref[j:j + 1, :]
    xbuf_ref[0:halo, :] = xbuf_ref[ts:ts + halo, :]

    xcb = xc.astype(BF16)
    nblk = wa_ref.shape[0]
    rb = tc // nblk
    gate = lambda w_ref: jnp.concatenate(
        [jnp.dot(xcb[:, n * rb:(n + 1) * rb], w_ref[n], preferred_element_type=F32) for n in range(nblk)],
        axis=1)
    r = _sigmoid(gate(wa_ref) + ba_ref[...])
    i = _sigmoid(gate(wx_ref) + bx_ref[...])
    log_a = -RG_LRU_C * r * _softplus(-lam_ref[...])
    a = jnp.exp(log_a)
    u = (i * xc) * jnp.sqrt(-jnp.tanh(log_a) * (a * a + 1.0))
    a_cum, h = _linear_scan_rows(a, u)
    h = h + a_cum * h_ref[0:1, :]
    h_ref[0:1, :] = h[ts - 1:ts, :]
    o_ref[...] = (_gelu_tanh(ry_ref[...]) * h).astype(o_ref.dtype)


def _rglru_branch(g2, d_rnn, conv_w, conv_b, w_a, b_a, w_x, b_x, lam, *, ts=256, tc=512):
    s = g2.shape[0]
    ts, tc = _tile(s, ts), _tile(d_rnn, tc)
    rb = w_a.shape[-1]
    nblk = tc // rb
    ncb = d_rnn // tc
    vec = lambda v: v.reshape(1, d_rnn)
    vspec = pl.BlockSpec((1, tc), lambda c, t: (0, c))
    wspec = pl.BlockSpec((nblk, rb, rb), lambda c, t: (c, 0, 0))
    vmem = 2 * (2 * _nbytes((ts, tc), F32) + _nbytes((ts, tc), BF16) + 2 * _nbytes((nblk, rb, rb), BF16))
    vmem += 14 * _nbytes((ts, tc), F32)
    return pl.pallas_call(
        _rglru_body,
        grid=(ncb, s // ts),
        in_specs=[pl.BlockSpec((ts, tc), lambda c, t: (t, c)),
                  pl.BlockSpec((ts, tc), lambda c, t: (t, c + ncb)),
                  pl.BlockSpec((CONV_WIDTH, tc), lambda c, t: (0, c)), vspec,
                  wspec, vspec, wspec, vspec, vspec],
        out_specs=pl.BlockSpec((ts, tc), lambda c, t: (t, c)),
        out_shape=jax.ShapeDtypeStruct((s, d_rnn), BF16),
        scratch_shapes=[pltpu.VMEM((ts + SUBLANES, tc), F32), pltpu.VMEM((SUBLANES, tc), F32)],
        compiler_params=_compiler_params(("parallel", "arbitrary"), vmem),
        name="rglru_branch",
    )(g2, g2, conv_w, vec(conv_b), w_a, vec(b_a), w_x, vec(b_x), vec(lam))


def _merge_body(att_ref, wo_a_ref, rnn_ref, wo_r_ref, ga_ref, gr_ref, o_ref):
    att = jnp.dot(att_ref[...], wo_a_ref[...], preferred_element_type=F32)
    rnn = jnp.dot(rnn_ref[...], wo_r_ref[...], preferred_element_type=F32)
    o_ref[...] = (_sigmoid(ga_ref[...]) * att + _sigmoid(gr_ref[...]) * rnn).astype(o_ref.dtype)


def _merge_branches(att, w_att_o, rnn, w_rnn_o, layer, g2, gate_col0, *, tm=1024, tn=512):
    m, ka = att.shape
    kr = rnn.shape[1]
    d = w_att_o.shape[2]
    tm, tn = _tile(m, tm), _tile(d, tn)
    ga0, gr0 = gate_col0 // tn, (gate_col0 + d) // tn
    vmem = 2 * (_nbytes((tm, ka + kr), BF16) + _nbytes((ka + kr, tn), BF16) + 2 * _nbytes((tm, tn), F32)
                + _nbytes((tm, tn), BF16)) + 3 * _nbytes((tm, tn), F32)
    return pl.pallas_call(
        _merge_body,
        grid=(m // tm, d // tn),
        in_specs=[pl.BlockSpec((tm, ka), lambda i, j: (i, 0)),
                  pl.BlockSpec((None, ka, tn), lambda i, j: (layer, 0, j)),
                  pl.BlockSpec((tm, kr), lambda i, j: (i, 0)),
                  pl.BlockSpec((None, kr, tn), lambda i, j: (layer, 0, j)),
                  pl.BlockSpec((tm, tn), lambda i, j: (i, j + ga0)),
                  pl.BlockSpec((tm, tn), lambda i, j: (i, j + gr0))],
        out_specs=pl.BlockSpec((tm, tn), lambda i, j: (i, j)),
        out_shape=jax.ShapeDtypeStruct((m, d), BF16),
        compiler_params=_compiler_params(("parallel", "arbitrary"), vmem),
        name="merge_branches",
    )(att, w_att_o, rnn, w_rnn_o, g2, g2)


def _glu_body(a_ref, wg_ref, wu_ref, o_ref):
    a = a_ref[...]
    g = jnp.dot(a, wg_ref[...], preferred_element_type=F32)
    u = jnp.dot(a, wu_ref[...], preferred_element_type=F32)
    o_ref[...] = (g * _sigmoid(g) * u).astype(o_ref.dtype)


def _glu_vmem(tm, k, tn):
    return 2 * (_nbytes((tm, k), BF16) + 2 * _nbytes((k, tn), BF16) + _nbytes((tm, tn), BF16)
                + _nbytes((tm, LANES), F32)) + 3 * _nbytes((tm, tn), F32)


def _dense_glu(xb, w_in, layer, *, tm=1024, tn=512):
    m, k = xb.shape
    f = w_in.shape[2] // 2
    tm, tn = _tile(m, tm), _tile(f, tn)
    nj = f // tn
    return pl.pallas_call(
        _glu_body,
        grid=(m // tm, nj),
        in_specs=[pl.BlockSpec((tm, k), lambda i, j: (i, 0)),
                  pl.BlockSpec((None, k, tn), lambda i, j: (layer, 0, j)),
                  pl.BlockSpec((None, k, tn), lambda i, j: (layer, 0, j + nj))],
        out_specs=pl.BlockSpec((tm, tn), lambda i, j: (i, j)),
        out_shape=jax.ShapeDtypeStruct((m, f), BF16),
        compiler_params=_compiler_params(("parallel", "arbitrary"), _glu_vmem(tm, k, tn)),
        name="dense_glu",
    )(xb, w_in, w_in)


def _router_body(x_ref, w_ref, meta_ref, wts_ref, cnt_ref, carry_ref, *, n_exp):
    @pl.when(pl.program_id(0) == 0)
    def _():
        carry_ref[...] = jnp.zeros_like(carry_ref)

    logits = jnp.dot(x_ref[...], w_ref[...], preferred_element_type=F32, precision=lax.Precision.HIGHEST)
    tm = logits.shape[0]
    lane = lax.broadcasted_iota(jnp.int32, logits.shape, 1)
    logits = jnp.where(lane < n_exp, logits, NEG_INF)

    def take_top(vals):
        top = jnp.max(vals, axis=1, keepdims=True)
        idx = jnp.min(jnp.where(vals == top, lane, LANES), axis=1, keepdims=True)
        return top, idx

    v1, i1 = take_top(logits)
    v2, i2 = take_top(jnp.where(lane == i1, NEG_INF, logits))
    e2 = jnp.exp(v2 - v1)
    denom = 1.0 + e2
    w1, w2 = 1.0 / denom, e2 / denom

    hot1 = (lane == i1).astype(F32)
    hot2 = (lane == i2).astype(F32)
    both = hot1 + hot2
    before = _cumsum_rows(both) - both + carry_ref[0:1, :]
    r1 = jnp.sum(hot1 * before, axis=1, keepdims=True).astype(jnp.int32)
    r2 = jnp.sum(hot2 * before, axis=1, keepdims=True).astype(jnp.int32)
    total = before[tm - 1:tm, :] + both[tm - 1:tm, :]
    carry_ref[0:1, :] = total
    cnt_ref[...] = jnp.broadcast_to(total, cnt_ref.shape)

    meta_ref[...] = jnp.where(lane == 0, i1, jnp.where(lane == 1, i2, jnp.where(lane == 2, r1,
                              jnp.where(lane == 3, r2, 0))))
    wts_ref[:, :LANES] = jnp.broadcast_to(w1, (tm, LANES))
    wts_ref[:, LANES:] = jnp.broadcast_to(w2, (tm, LANES))


def _router(x, w_router, *, tm=512):
    m, d = x.shape
    n_exp = w_router.shape[1]
    tm = _tile(m, tm)
    w_pad = jnp.pad(w_router, ((0, 0), (0, LANES - n_exp)))
    vmem = 2 * (_nbytes((tm, d), F32) + _nbytes((d, LANES), F32) + 3 * _nbytes((tm, LANES), F32))
    vmem += 16 * _nbytes((tm, LANES), F32) + 3 * _nbytes((tm, d), F32)
    return pl.pallas_call(
        functools.partial(_router_body, n_exp=n_exp),
        grid=(m // tm,),
        in_specs=[pl.BlockSpec((tm, d), lambda i: (i, 0)),
                  pl.BlockSpec((d, LANES), lambda i: (0, 0))],
        out_specs=[pl.BlockSpec((tm, LANES), lambda i: (i, 0)),
                   pl.BlockSpec((tm, 2 * LANES), lambda i: (i, 0)),
                   pl.BlockSpec((SUBLANES, LANES), lambda i: (0, 0))],
        out_shape=[jax.ShapeDtypeStruct((m, LANES), jnp.int32),
                   jax.ShapeDtypeStruct((m, 2 * LANES), F32),
                   jax.ShapeDtypeStruct((SUBLANES, LANES), F32)],
        scratch_shapes=[pltpu.VMEM((SUBLANES, LANES), F32)],
        compiler_params=_compiler_params(("arbitrary",), vmem),
        name="router_top2",
    )(x, w_pad)


def _routing_tables(meta, counts, n_exp, tile, n_tiles):
    e1, e2, r1, r2 = meta[:, 0], meta[:, 1], meta[:, 2], meta[:, 3]
    cnt = counts[0, :n_exp].astype(jnp.int32)
    padded = (cnt + tile - 1) // tile * tile
    ends = jnp.cumsum(padded)
    starts = ends - padded
    pos1 = starts[e1] + r1
    pos2 = starts[e2] + r2
    tile_start = jnp.arange(n_tiles, dtype=jnp.int32) * tile
    tile_expert = jnp.minimum(jnp.sum(tile_start[:, None] >= ends[None, :], axis=1), n_exp - 1).astype(jnp.int32)
    n_used = (ends[n_exp - 1] // tile).astype(jnp.int32).reshape(1)
    return pos1, pos2, tile_expert, n_used


def _row_copies(jobs, n_rows):
    def copy(job, r):
        src_ref, dst_ref, sem, src_row, dst_row = job
        return pltpu.make_async_copy(src_ref.at[pl.ds(src_row(r), 1), :], dst_ref.at[pl.ds(dst_row(r), 1), :], sem)

    def start(r, carry):
        for job in jobs:
            copy(job, r).start()
        return carry

    def wait(r, carry):
        for job in jobs:
            copy(job, r).wait()
        return carry

    lax.fori_loop(0, n_rows, start, 0)
    lax.fori_loop(0, n_rows, wait, 0)


def _dispatch_body(pos1_ref, pos2_ref, x_hbm, xs_in_hbm, xs_hbm, sem):
    del xs_in_hbm
    tm = pos1_ref.shape[1]
    t0 = pl.program_id(0) * tm
    token = lambda r: t0 + r
    _row_copies([(x_hbm, xs_hbm, sem.at[0], token, lambda r: pos1_ref[0, r]),
                 (x_hbm, xs_hbm, sem.at[1], token, lambda r: pos2_ref[0, r])], tm)


def _dispatch(x, pos1, pos2, n_rows, *, tm=512):
    m, d = x.shape
    tm = _tile(m, tm)
    pos_spec = pl.BlockSpec((None, 1, tm), lambda i: (i, 0, 0), memory_space=pltpu.SMEM)
    any_spec = pl.BlockSpec(memory_space=pl.ANY)
    return pl.pallas_call(
        _dispatch_body,
        grid=(m // tm,),
        in_specs=[pos_spec, pos_spec, any_spec, any_spec],
        out_specs=any_spec,
        out_shape=jax.ShapeDtypeStruct((n_rows, d), x.dtype),
        scratch_shapes=[pltpu.SemaphoreType.DMA((2,))],
        input_output_aliases={3: 0},
        compiler_params=pltpu.CompilerParams(dimension_semantics=("arbitrary",)),
        name="moe_dispatch",
    )(pos1.reshape(m // tm, 1, tm), pos2.reshape(m // tm, 1, tm), x, jnp.zeros((n_rows, d), x.dtype))


def _grouped_glu_body(te_ref, nu_ref, a_ref, wg_ref, wu_ref, o_ref, ab_ref):
    del te_ref
    used = pl.program_id(0) < nu_ref[0]

    @pl.when(used)
    def _():
        @pl.when(pl.program_id(1) == 0)
        def _():
            ab_ref[...] = a_ref[...].astype(BF16)

        a = ab_ref[...]
        g = jnp.dot(a, wg_ref[...], preferred_element_type=F32)
        u = jnp.dot(a, wu_ref[...], preferred_element_type=F32)
        o_ref[...] = (g * _sigmoid(g) * u).astype(o_ref.dtype)

    @pl.when(jnp.logical_not(used))
    def _():
        o_ref[...] = jnp.zeros_like(o_ref)


def _grouped_glu(xs, w_exp_in, layer, tile_expert, n_used, *, tile, tn=512):
    p, k = xs.shape
    f = w_exp_in.shape[3] // 2
    tn = _tile(f, tn)
    r = f // tn
    vmem = 2 * (_nbytes((tile, k), F32) + 2 * _nbytes((k, tn), BF16) + _nbytes((tile, tn), BF16))
    vmem += _nbytes((tile, k), BF16) + 3 * _nbytes((tile, tn), F32)
    return pl.pallas_call(
        _grouped_glu_body,
        grid_spec=pltpu.PrefetchScalarGridSpec(
            num_scalar_prefetch=2,
            grid=(p // tile, r),
            in_specs=[pl.BlockSpec((tile, k), lambda i, j, te, nu: (i, 0)),
                      pl.BlockSpec((None, None, k, tn), lambda i, j, te, nu: (layer, te[i], 0, j)),
                      pl.BlockSpec((None, None, k, tn), lambda i, j, te, nu: (layer, te[i], 0, j + r))],
            out_specs=pl.BlockSpec((tile, tn), lambda i, j, te, nu: (i, j)),
            scratch_shapes=[pltpu.VMEM((tile, k), BF16)]),
        out_shape=jax.ShapeDtypeStruct((p, f), BF16),
        compiler_params=_compiler_params(("arbitrary", "arbitrary"), vmem),
        name="expert_glu",
    )(tile_expert, n_used, xs, w_exp_in, w_exp_in)


def _grouped_out_body(te_ref, nu_ref, a_ref, w_ref, o_ref):
    del te_ref
    used = pl.program_id(0) < nu_ref[0]

    @pl.when(used)
    def _():
        o_ref[...] = jnp.dot(a_ref[...], w_ref[...], preferred_element_type=F32)

    @pl.when(jnp.logical_not(used))
    def _():
        o_ref[...] = jnp.zeros_like(o_ref)


def _grouped_out(act, w_exp_out, layer, tile_expert, n_used, *, tile):
    p, f = act.shape
    d = w_exp_out.shape[3]
    vmem = 2 * (_nbytes((tile, f), BF16) + _nbytes((f, d), BF16) + _nbytes((tile, d), F32))
    vmem += _nbytes((tile, d), F32)
    return pl.pallas_call(
        _grouped_out_body,
        grid_spec=pltpu.PrefetchScalarGridSpec(
            num_scalar_prefetch=2,
            grid=(p // tile,),
            in_specs=[pl.BlockSpec((tile, f), lambda i, te, nu: (i, 0)),
                      pl.BlockSpec((None, None, f, d), lambda i, te, nu: (layer, te[i], 0, 0))],
            out_specs=pl.BlockSpec((tile, d), lambda i, te, nu: (i, 0))),
        out_shape=jax.ShapeDtypeStruct((p, d), F32),
        compiler_params=_compiler_params(("arbitrary",), vmem),
        name="expert_out",
    )(tile_expert, n_used, act, w_exp_out)


def _combine_ln_body(pos1_ref, pos2_ref, x_ref, wts_ref, g_ref, b_ref, y_hbm, o_ref, ob_ref, y1_ref, y2_ref, sem,
                     *, alpha):
    tm, d = x_ref.shape
    local = lambda r: r
    _row_copies([(y_hbm, y1_ref, sem.at[0], lambda r: pos1_ref[0, r], local),
                 (y_hbm, y2_ref, sem.at[1], lambda r: pos2_ref[0, r], local)], tm)
    w1 = jnp.tile(wts_ref[:, :LANES], (1, d // LANES))
    w2 = jnp.tile(wts_ref[:, LANES:], (1, d // LANES))
    ff = w1 * y1_ref[...] + w2 * y2_ref[...]
    z = alpha * x_ref[...] + ff
    mu = jnp.mean(z, axis=-1, keepdims=True)
    zc = z - mu
    var = jnp.mean(zc * zc, axis=-1, keepdims=True)
    o = zc * lax.rsqrt(var + LN_EPS) * g_ref[...] + b_ref[...]
    o_ref[...] = o
    ob_ref[...] = o.astype(BF16)


def _combine_ln(x, y, pos1, pos2, wts, g, b, alpha, *, tm=256):
    m, d = x.shape
    tm = _tile(m, tm)
    row = pl.BlockSpec((tm, d), lambda i: (i, 0))
    vec = pl.BlockSpec((1, d), lambda i: (0, 0))
    pos_spec = pl.BlockSpec((None, 1, tm), lambda i: (i, 0, 0), memory_space=pltpu.SMEM)
    vmem = 2 * (2 * _nbytes((tm, d), F32) + _nbytes((tm, d), BF16) + _nbytes((tm, 2 * LANES), F32))
    vmem += 2 * _nbytes((tm, d), F32) + 4 * _nbytes((tm, d), F32)
    return pl.pallas_call(
        functools.partial(_combine_ln_body, alpha=alpha),
        grid=(m // tm,),
        in_specs=[pos_spec, pos_spec, row, pl.BlockSpec((tm, 2 * LANES), lambda i: (i, 0)), vec, vec,
                  pl.BlockSpec(memory_space=pl.ANY)],
        out_specs=[row, row],
        out_shape=[jax.ShapeDtypeStruct((m, d), F32), jax.ShapeDtypeStruct((m, d), BF16)],
        scratch_shapes=[pltpu.VMEM((tm, d), F32), pltpu.VMEM((tm, d), F32), pltpu.SemaphoreType.DMA((2,))],
        compiler_params=_compiler_params(("arbitrary",), vmem),
        name="moe_combine_ln",
    )(pos1.reshape(m // tm, 1, tm), pos2.reshape(m // tm, 1, tm), x, wts, g.reshape(1, d), b.reshape(1, d), y)


def kernel(x, w_in, b_f, conv_w, conv_b, w_a, b_a, w_x, b_x, lam, w_att_o, w_rnn_o, w_out,
           ln1_g, ln1_b, w_ffn_in, w_ffn_out, w_router, w_exp_in, w_exp_out, ln2_g, ln2_b):
    batch, seq, d_model = x.shape
    depth = w_in.shape[0]
    n_heads = b_f.shape[1]
    d_att = w_att_o.shape[1]
    d_rnn = w_rnn_o.shape[1]
    assert d_att == n_heads * LANES and w_a.shape[-1] == LANES
    assert w_in.shape[2] == 3 * d_att + n_heads + 2 * d_rnn + 2 * d_model
    alpha = float((2 * depth) ** 0.25)
    attn_blk = _tile(seq, 512)
    q_scale = float(LANES ** -0.5) * LOG2_E

    f_col0 = 3 * d_att
    g2_col0 = f_col0 + n_heads
    w_in_b = w_in.astype(BF16)
    w_f = w_in_b[:, :, f_col0:g2_col0]
    w_g2 = w_in_b[:, :, g2_col0:]
    w_a, w_x = w_a.astype(BF16), w_x.astype(BF16)
    w_att_o, w_rnn_o, w_out = w_att_o.astype(BF16), w_rnn_o.astype(BF16), w_out.astype(BF16)
    w_ffn_in, w_ffn_out = w_ffn_in.astype(BF16), w_ffn_out.astype(BF16)
    w_exp_in = w_exp_in.astype(BF16)
    w_exp_out = w_exp_out.astype(BF16)
    n_exp = w_router.shape[2]
    moe_rows = seq * TOP_K + n_exp * MOE_TILE

    outs = []
    for bi in range(batch):
        xf = x[bi]
        xb = xf.astype(BF16)
        for l in range(depth):
            qkv = _matmul(xb, w_in_b, l, BF16, n=f_col0, scaled_cols=d_att, col_scale=q_scale, name="proj_qkv")
            g2 = _matmul(xb, w_g2, l, F32, name="proj_rnn_gates")
            c = _decay_cumsum(xb, w_f[l], b_f[l], ts=attn_blk)
            c = c.reshape(n_heads, seq // attn_blk, attn_blk)
            att = _forgetting_attention(qkv, c, n_heads, blk=attn_blk)
            rnn = _rglru_branch(g2, d_rnn, conv_w[l], conv_b[l], w_a[l], b_a[l], w_x[l], b_x[l], lam[l])
            merged = _merge_branches(att, w_att_o, rnn, w_rnn_o, l, g2, 2 * d_rnn)
            mix = _matmul(merged, w_out, l, F32, name="proj_out")
            xf, xb = _deepnorm_ln(xf, mix, ln1_g[l], ln1_b[l], alpha)
            j = l // 2
            if l % 2 == 0:
                act = _dense_glu(xb, w_ffn_in, j)
                ff = _matmul(act, w_ffn_out, j, F32, tm=512, name="ffn_out")
                xf, xb = _deepnorm_ln(xf, ff, ln2_g[l], ln2_b[l], alpha)
            else:
                meta, wts, counts = _router(xf, w_router[j])
                pos1, pos2, tile_expert, n_used = _routing_tables(meta, counts, n_exp, MOE_TILE, moe_rows // MOE_TILE)
                xs = _dispatch(xf, pos1, pos2, moe_rows)
                act = _grouped_glu(xs, w_exp_in, j, tile_expert, n_used, tile=MOE_TILE)
                y = _grouped_out(act, w_exp_out, j, tile_expert, n_used, tile=MOE_TILE)
                xf, xb = _combine_ln(xf, y, pos1, pos2, wts, ln2_g[l], ln2_b[l], alpha)
        outs.append(xf)
    return jnp.stack(outs)
```

```python
import functools

import numpy as np
import jax
import jax.numpy as jnp
from jax import lax
from jax.experimental import pallas as pl
from jax.experimental.pallas import tpu as pltpu

F32 = jnp.float32
BF16 = jnp.bfloat16

LANES = 128
SUBLANES = 8
V7X_VMEM_BYTES = 64 * 2 ** 20
VMEM_RESERVE_BYTES = 8 * 2 ** 20

CONV_WIDTH = 4
RG_LRU_C = 8.0
TOP_K = 2
LN_EPS = 1e-5
NEG_INF = float("-inf")
LOG2_E = float(np.log2(np.e))
HEADS_PER_STEP = 2
MOE_TILE = 512


def _compiler_params(semantics, block_bytes):
    limit = min(int(block_bytes) + VMEM_RESERVE_BYTES, V7X_VMEM_BYTES - VMEM_RESERVE_BYTES)
    return pltpu.CompilerParams(dimension_semantics=semantics, vmem_limit_bytes=limit)


def _nbytes(shape, dtype):
    return int(np.prod(shape)) * jnp.dtype(dtype).itemsize


def _tile(dim, pref):
    if dim <= pref:
        return dim
    t = pref - pref % LANES
    while dim % t:
        t -= LANES
    assert t > 0, (dim, pref)
    return t


def _mm_body(a_ref, b_ref, o_ref, *, scaled_cols, col_scale):
    acc = jnp.dot(a_ref[...], b_ref[...], preferred_element_type=F32)
    if scaled_cols:
        acc = acc * jnp.where(pl.program_id(1) * o_ref.shape[1] < scaled_cols, col_scale, 1.0)
    o_ref[...] = acc.astype(o_ref.dtype)


def _matmul(a, b, layer, out_dtype, *, tm=1024, tn=512, col0=0, n=None, scaled_cols=0, col_scale=1.0, name):
    m, k = a.shape
    n = b.shape[2] - col0 if n is None else n
    tm, tn = _tile(m, tm), _tile(n, tn)
    assert scaled_cols % tn == 0 and col0 % tn == 0
    jb0 = col0 // tn
    vmem = 2 * (_nbytes((tm, k), a.dtype) + _nbytes((k, tn), b.dtype) + _nbytes((tm, tn), out_dtype))
    vmem += _nbytes((tm, tn), F32)
    return pl.pallas_call(
        functools.partial(_mm_body, scaled_cols=scaled_cols, col_scale=col_scale),
        grid=(m // tm, n // tn),
        in_specs=[pl.BlockSpec((tm, k), lambda i, j: (i, 0)),
                  pl.BlockSpec((None, k, tn), lambda i, j: (layer, 0, j + jb0))],
        out_specs=pl.BlockSpec((tm, tn), lambda i, j: (i, j)),
        out_shape=jax.ShapeDtypeStruct((m, n), out_dtype),
        compiler_params=_compiler_params(("parallel", "arbitrary"), vmem),
        name=name,
    )(a, b)


def _ln_body(x_ref, y_ref, g_ref, b_ref, o_ref, ob_ref, *, alpha):
    z = alpha * x_ref[...] + y_ref[...]
    mu = jnp.mean(z, axis=-1, keepdims=True)
    zc = z - mu
    var = jnp.mean(zc * zc, axis=-1, keepdims=True)
    y = zc * lax.rsqrt(var + LN_EPS)
    o = y * g_ref[...] + b_ref[...]
    o_ref[...] = o
    ob_ref[...] = o.astype(BF16)


def _deepnorm_ln(x, y, g, b, alpha, *, tm=256):
    m, d = x.shape
    tm = _tile(m, tm)
    row = pl.BlockSpec((tm, d), lambda i: (i, 0))
    vec = pl.BlockSpec((1, d), lambda i: (0, 0))
    vmem = 2 * (3 * _nbytes((tm, d), F32) + _nbytes((tm, d), BF16)) + 3 * _nbytes((tm, d), F32)
    return pl.pallas_call(
        functools.partial(_ln_body, alpha=alpha),
        grid=(m // tm,),
        in_specs=[row, row, vec, vec],
        out_specs=[row, row],
        out_shape=[jax.ShapeDtypeStruct((m, d), F32), jax.ShapeDtypeStruct((m, d), BF16)],
        compiler_params=_compiler_params(("parallel",), vmem),
        name="deepnorm_ln",
    )(x, y, g.reshape(1, d), b.reshape(1, d))


def _shift_rows(x, d, fill):
    rows = lax.broadcasted_iota(jnp.int32, x.shape, 0)
    return jnp.where(rows >= d, pltpu.roll(x, d, 0), fill)


def _linear_scan_rows(a, b):
    d = 1
    while d < a.shape[0]:
        b = a * _shift_rows(b, d, 0.0) + b
        a = a * _shift_rows(a, d, 1.0)
        d *= 2
    return a, b


def _cumsum_rows(x):
    d = 1
    while d < x.shape[0]:
        x = x + _shift_rows(x, d, 0.0)
        d *= 2
    return x


def _sigmoid(z):
    return 1.0 / (1.0 + jnp.exp(-z))


def _log_sigmoid(z):
    return jnp.minimum(z, 0.0) - jnp.log1p(jnp.exp(-jnp.abs(z)))


def _softplus(z):
    return jnp.maximum(z, 0.0) + jnp.log1p(jnp.exp(-jnp.abs(z)))


def _decay_body(x_ref, w_ref, bf_ref, o_ref, carry_ref):
    @pl.when(pl.program_id(0) == 0)
    def _():
        carry_ref[...] = jnp.zeros_like(carry_ref)

    z = jnp.dot(x_ref[...], w_ref[...], preferred_element_type=F32) + bf_ref[...]
    c = _cumsum_rows(_log_sigmoid(z)) + carry_ref[0:1, :]
    t = c.shape[0]
    carry_ref[0:1, :] = c[t - 1:t, :]
    o_ref[...] = (c * LOG2_E).T[:o_ref.shape[0], :]


def _decay_cumsum(xb, w_f, b_f, *, ts):
    s, d = xb.shape
    hp = max(SUBLANES, -(-b_f.shape[0] // SUBLANES) * SUBLANES)
    pad = LANES - b_f.shape[0]
    w_pad = jnp.pad(w_f, ((0, 0), (0, pad)))
    b_pad = jnp.pad(b_f, (0, pad)).reshape(1, LANES)
    vmem = 2 * (_nbytes((ts, d), BF16) + _nbytes((d, LANES), BF16) + _nbytes((hp, ts), F32))
    vmem += 6 * _nbytes((ts, LANES), F32)
    out = pl.pallas_call(
        _decay_body,
        grid=(s // ts,),
        in_specs=[pl.BlockSpec((ts, d), lambda t: (t, 0)),
                  pl.BlockSpec((d, LANES), lambda t: (0, 0)),
                  pl.BlockSpec((1, LANES), lambda t: (0, 0))],
        out_specs=pl.BlockSpec((hp, ts), lambda t: (0, t)),
        out_shape=jax.ShapeDtypeStruct((hp, s), F32),
        scratch_shapes=[pltpu.VMEM((SUBLANES, LANES), F32)],
        compiler_params=_compiler_params(("arbitrary",), vmem),
        name="decay_cumsum",
    )(xb, w_pad, b_pad)
    return out[:b_f.shape[0]]


def _attn_body(q_ref, k_ref, v_ref, c_ref, o_ref, m_ref, l_ref, acc_ref, p_ref, corr_ref, *, blk):
    i = pl.program_id(1)
    dh = LANES
    n_sub = q_ref.shape[1] // dh
    reps = blk // LANES
    ones_col = (lax.broadcasted_iota(jnp.int32, (blk, dh), 1) == 0).astype(BF16)

    def key_rows(j):
        return pl.ds(pl.multiple_of(j * blk, blk), blk)

    def scores(j, slot, masked=False):
        for a in range(n_sub):
            cols = slice(a * dh, (a + 1) * dh)
            s = lax.dot_general(q_ref[:, cols], k_ref[key_rows(j), cols], (((1,), (1,)), ((), ())),
                                preferred_element_type=F32)
            s = s - c_ref[a, pl.ds(j, 1), :]
            if masked:
                qpos = lax.broadcasted_iota(jnp.int32, s.shape, 0)
                kpos = lax.broadcasted_iota(jnp.int32, s.shape, 1)
                s = jnp.where(kpos <= qpos, s, NEG_INF)
            m_prev = m_ref[a]
            m_new = jnp.maximum(m_prev, jnp.max(s, axis=1, keepdims=True))
            p_ref[slot, a] = jnp.exp2(s - jnp.tile(m_new, (1, reps))).astype(BF16)
            corr_ref[slot, a] = jnp.exp2(m_prev - m_new)
            m_ref[a] = m_new

    def values(j, slot):
        for a in range(n_sub):
            v = v_ref[key_rows(j), a * dh:(a + 1) * dh]
            pv = jnp.dot(p_ref[slot, a], jnp.concatenate([v, ones_col], axis=1), preferred_element_type=F32)
            corr = corr_ref[slot, a]
            acc_ref[a] = corr * acc_ref[a] + pv[:, :dh]
            l_ref[a] = corr * l_ref[a] + pv[:, dh:]

    m_ref[...] = jnp.full_like(m_ref, NEG_INF)
    l_ref[...] = jnp.zeros_like(l_ref)
    acc_ref[...] = jnp.zeros_like(acc_ref)
    scores(i, 0, masked=True)
    n_pairs = i // 2

    def pair(u, carry):
        scores(2 * u, 1)
        values(jnp.where(u == 0, i, 2 * u - 1), 0)
        scores(2 * u + 1, 0)
        values(2 * u, 1)
        return carry

    lax.fori_loop(0, n_pairs, pair, 0)
    parked = jnp.where(n_pairs == 0, i, 2 * n_pairs - 1)

    @pl.when(i % 2 == 1)
    def _():
        scores(i - 1, 1)
        values(parked, 0)
        values(i - 1, 1)

    @pl.when(i % 2 == 0)
    def _():
        values(parked, 0)

    for a in range(n_sub):
        l = jnp.sum(l_ref[a], axis=1, keepdims=True)
        o_ref[:, a * dh:(a + 1) * dh] = (acc_ref[a] / l).astype(o_ref.dtype)


def _forgetting_attention(qkv, c, n_heads, *, blk):
    s = qkv.shape[0]
    dh = LANES
    nb = s // blk
    hps = HEADS_PER_STEP if n_heads % HEADS_PER_STEP == 0 else 1
    w = hps * dh
    groups = n_heads // hps
    head_cols = lambda off: pl.BlockSpec((s, w), lambda h, i: (0, off + h))
    vmem = 2 * (2 * _nbytes((s, w), BF16) + 2 * _nbytes((blk, w), BF16) + _nbytes((hps, nb, blk), F32))
    acc_like = pltpu.VMEM((hps, blk, dh), F32)
    p_slots = pltpu.VMEM((2, hps, blk, blk), BF16)
    corr_slots = pltpu.VMEM((2, hps, blk, dh), F32)
    vmem += 5 * _nbytes((hps, blk, dh), F32) + _nbytes((2, hps, blk, blk), BF16)
    vmem += 4 * hps * _nbytes((blk, blk), F32)
    return pl.pallas_call(
        functools.partial(_attn_body, blk=blk),
        grid=(groups, nb),
        in_specs=[pl.BlockSpec((blk, w), lambda h, i: (i, h)),
                  head_cols(groups), head_cols(2 * groups),
                  pl.BlockSpec((hps, nb, blk), lambda h, i: (h, 0, 0))],
        out_specs=pl.BlockSpec((blk, w), lambda h, i: (i, h)),
        out_shape=jax.ShapeDtypeStruct((s, n_heads * dh), BF16),
        scratch_shapes=[acc_like, acc_like, acc_like, p_slots, corr_slots],
        compiler_params=_compiler_params(("parallel", "arbitrary"), vmem),
        name="forgetting_attention",
    )(qkv, qkv, qkv, c)


def _gelu_tanh(x):
    z = np.float32(np.sqrt(2.0 / np.pi)) * (x + 0.044715 * (x * x * x))
    return x * _sigmoid(2.0 * z)


def _rglru_body(rx_ref, ry_ref, cw_ref, cb_ref, wa_ref, ba_ref, wx_ref, bx_ref, lam_ref,
                o_ref, xbuf_ref, h_ref):
    ts, tc = rx_ref.shape
    halo = SUBLANES

    @pl.when(pl.program_id(1) == 0)
    def _():
        xbuf_ref[0:halo, :] = jnp.zeros((halo, tc), F32)
        h_ref[...] = jnp.zeros_like(h_ref)

    xbuf_ref[halo:halo + ts, :] = rx_ref[...]
    xc = cb_ref[...]
    for j in range(CONV_WIDTH):
        off = halo - (CONV_WIDTH - 1) + j
        xc = xc + xbuf_ref[off:off + ts, :] * cw_ref[j:j + 1, :]
    xbuf_ref[0:halo, :] = xbuf_ref[ts:ts + halo, :]

    xcb = xc.astype(BF16)
    nblk = wa_ref.shape[0]
    rb = tc // nblk
    gate = lambda w_ref: jnp.concatenate(
        [jnp.dot(xcb[:, n * rb:(n + 1) * rb], w_ref[n], preferred_element_type=F32) for n in range(nblk)],
        axis=1)
    r = _sigmoid(gate(wa_ref) + ba_ref[...])
    i = _sigmoid(gate(wx_ref) + bx_ref[...])
    log_a = -RG_LRU_C * r * _softplus(-lam_ref[...])
    a = jnp.exp(log_a)
    u = (i * xc) * jnp.sqrt(-jnp.tanh(log_a) * (a * a + 1.0))
    a_cum, h = _linear_scan_rows(a, u)
    h = h + a_cum * h_ref[0:1, :]
    h_ref[0:1, :] = h[ts - 1:ts, :]
    o_ref[...] = (_gelu_tanh(ry_ref[...]) * h).astype(o_ref.dtype)


def _rglru_branch(g2, d_rnn, conv_w, conv_b, w_a, b_a, w_x, b_x, lam, *, ts=256, tc=512):
    s = g2.shape[0]
    ts, tc = _tile(s, ts), _tile(d_rnn, tc)
    rb = w_a.shape[-1]
    nblk = tc // rb
    ncb = d_rnn // tc
    vec = lambda v: v.reshape(1, d_rnn)
    vspec = pl.BlockSpec((1, tc), lambda c, t: (0, c))
    wspec = pl.BlockSpec((nblk, rb, rb), lambda c, t: (c, 0, 0))
    vmem = 2 * (2 * _nbytes((ts, tc), F32) + _nbytes((ts, tc), BF16) + 2 * _nbytes((nblk, rb, rb), BF16))
    vmem += 14 * _nbytes((ts, tc), F32)
    return pl.pallas_call(
        _rglru_body,
        grid=(ncb, s // ts),
        in_specs=[pl.BlockSpec((ts, tc), lambda c, t: (t, c)),
                  pl.BlockSpec((ts, tc), lambda c, t: (t, c + ncb)),
                  pl.BlockSpec((CONV_WIDTH, tc), lambda c, t: (0, c)), vspec,
                  wspec, vspec, wspec, vspec, vspec],
        out_specs=pl.BlockSpec((ts, tc), lambda c, t: (t, c)),
        out_shape=jax.ShapeDtypeStruct((s, d_rnn), BF16),
        scratch_shapes=[pltpu.VMEM((ts + SUBLANES, tc), F32), pltpu.VMEM((SUBLANES, tc), F32)],
        compiler_params=_compiler_params(("parallel", "arbitrary"), vmem),
        name="rglru_branch",
    )(g2, g2, conv_w, vec(conv_b), w_a, vec(b_a), w_x, vec(b_x), vec(lam))


def _merge_body(att_ref, wo_a_ref, rnn_ref, wo_r_ref, ga_ref, gr_ref, o_ref):
    att = jnp.dot(att_ref[...], wo_a_ref[...], preferred_element_type=F32)
    rnn = jnp.dot(rnn_ref[...], wo_r_ref[...], preferred_element_type=F32)
    o_ref[...] = (_sigmoid(ga_ref[...]) * att + _sigmoid(gr_ref[...]) * rnn).astype(o_ref.dtype)


def _merge_branches(att, w_att_o, rnn, w_rnn_o, layer, g2, gate_col0, *, tm=1024, tn=512):
    m, ka = att.shape
    kr = rnn.shape[1]
    d = w_att_o.shape[2]
    tm, tn = _tile(m, tm), _tile(d, tn)
    ga0, gr0 = gate_col0 // tn, (gate_col0 + d) // tn
    vmem = 2 * (_nbytes((tm, ka + kr), BF16) + _nbytes((ka + kr, tn), BF16) + 2 * _nbytes((tm, tn), F32)
                + _nbytes((tm, tn), BF16)) + 3 * _nbytes((tm, tn), F32)
    return pl.pallas_call(
        _merge_body,
        grid=(m // tm, d // tn),
        in_specs=[pl.BlockSpec((tm, ka), lambda i, j: (i, 0)),
                  pl.BlockSpec((None, ka, tn), lambda i, j: (layer, 0, j)),
                  pl.BlockSpec((tm, kr), lambda i, j: (i, 0)),
                  pl.BlockSpec((None, kr, tn), lambda i, j: (layer, 0, j)),
                  pl.BlockSpec((tm, tn), lambda i, j: (i, j + ga0)),
                  pl.BlockSpec((tm, tn), lambda i, j: (i, j + gr0))],
        out_specs=pl.BlockSpec((tm, tn), lambda i, j: (i, j)),
        out_shape=jax.ShapeDtypeStruct((m, d), BF16),
        compiler_params=_compiler_params(("parallel", "arbitrary"), vmem),
        name="merge_branches",
    )(att, w_att_o, rnn, w_rnn_o, g2, g2)


def _glu_body(a_ref, wg_ref, wu_ref, o_ref):
    a = a_ref[...]
    g = jnp.dot(a, wg_ref[...], preferred_element_type=F32)
    u = jnp.dot(a, wu_ref[...], preferred_element_type=F32)
    o_ref[...] = (g * _sigmoid(g) * u).astype(o_ref.dtype)


def _glu_vmem(tm, k, tn):
    return 2 * (_nbytes((tm, k), BF16) + 2 * _nbytes((k, tn), BF16) + _nbytes((tm, tn), BF16)
                + _nbytes((tm, LANES), F32)) + 3 * _nbytes((tm, tn), F32)


def _dense_glu(xb, w_in, layer, *, tm=1024, tn=512):
    m, k = xb.shape
    f = w_in.shape[2] // 2
    tm, tn = _tile(m, tm), _tile(f, tn)
    nj = f // tn
    return pl.pallas_call(
        _glu_body,
        grid=(m // tm, nj),
        in_specs=[pl.BlockSpec((tm, k), lambda i, j: (i, 0)),
                  pl.BlockSpec((None, k, tn), lambda i, j: (layer, 0, j)),
                  pl.BlockSpec((None, k, tn), lambda i, j: (layer, 0, j + nj))],
        out_specs=pl.BlockSpec((tm, tn), lambda i, j: (i, j)),
        out_shape=jax.ShapeDtypeStruct((m, f), BF16),
        compiler_params=_compiler_params(("parallel", "arbitrary"), _glu_vmem(tm, k, tn)),
        name="dense_glu",
    )(xb, w_in, w_in)


def _router_body(x_ref, w_ref, meta_ref, wts_ref, cnt_ref, carry_ref, *, n_exp):
    @pl.when(pl.program_id(0) == 0)
    def _():
        carry_ref[...] = jnp.zeros_like(carry_ref)

    logits = jnp.dot(x_ref[...], w_ref[...], preferred_element_type=F32, precision=lax.Precision.HIGHEST)
    tm = logits.shape[0]
    lane = lax.broadcasted_iota(jnp.int32, logits.shape, 1)
    logits = jnp.where(lane < n_exp, logits, NEG_INF)

    def take_top(vals):
        top = jnp.max(vals, axis=1, keepdims=True)
        idx = jnp.min(jnp.where(vals == top, lane, LANES), axis=1, keepdims=True)
        return top, idx

    v1, i1 = take_top(logits)
    v2, i2 = take_top(jnp.where(lane == i1, NEG_INF, logits))
    e2 = jnp.exp(v2 - v1)
    denom = 1.0 + e2
    w1, w2 = 1.0 / denom, e2 / denom

    hot1 = (lane == i1).astype(F32)
    hot2 = (lane == i2).astype(F32)
    both = hot1 + hot2
    before = _cumsum_rows(both) - both + carry_ref[0:1, :]
    r1 = jnp.sum(hot1 * before, axis=1, keepdims=True).astype(jnp.int32)
    r2 = jnp.sum(hot2 * before, axis=1, keepdims=True).astype(jnp.int32)
    total = before[tm - 1:tm, :] + both[tm - 1:tm, :]
    carry_ref[0:1, :] = total
    cnt_ref[...] = jnp.broadcast_to(total, cnt_ref.shape)

    meta_ref[...] = jnp.where(lane == 0, i1, jnp.where(lane == 1, i2, jnp.where(lane == 2, r1,
                              jnp.where(lane == 3, r2, 0))))
    wts_ref[:, :LANES] = jnp.broadcast_to(w1, (tm, LANES))
    wts_ref[:, LANES:] = jnp.broadcast_to(w2, (tm, LANES))


def _router(x, w_router, *, tm=512):
    m, d = x.shape
    n_exp = w_router.shape[1]
    tm = _tile(m, tm)
    w_pad = jnp.pad(w_router, ((0, 0), (0, LANES - n_exp)))
    vmem = 2 * (_nbytes((tm, d), F32) + _nbytes((d, LANES), F32) + 3 * _nbytes((tm, LANES), F32))
    vmem += 16 * _nbytes((tm, LANES), F32) + 3 * _nbytes((tm, d), F32)
    return pl.pallas_call(
        functools.partial(_router_body, n_exp=n_exp),
        grid=(m // tm,),
        in_specs=[pl.BlockSpec((tm, d), lambda i: (i, 0)),
                  pl.BlockSpec((d, LANES), lambda i: (0, 0))],
        out_specs=[pl.BlockSpec((tm, LANES), lambda i: (i, 0)),
                   pl.BlockSpec((tm, 2 * LANES), lambda i: (i, 0)),
                   pl.BlockSpec((SUBLANES, LANES), lambda i: (0, 0))],
        out_shape=[jax.ShapeDtypeStruct((m, LANES), jnp.int32),
                   jax.ShapeDtypeStruct((m, 2 * LANES), F32),
                   jax.ShapeDtypeStruct((SUBLANES, LANES), F32)],
        scratch_shapes=[pltpu.VMEM((SUBLANES, LANES), F32)],
        compiler_params=_compiler_params(("arbitrary",), vmem),
        name="router_top2",
    )(x, w_pad)


def _routing_tables(meta, counts, n_exp, tile, n_tiles):
    e1, e2, r1, r2 = meta[:, 0], meta[:, 1], meta[:, 2], meta[:, 3]
    cnt = counts[0, :n_exp].astype(jnp.int32)
    padded = (cnt + tile - 1) // tile * tile
    ends = jnp.cumsum(padded)
    starts = ends - padded
    pos1 = starts[e1] + r1
    pos2 = starts[e2] + r2
    tile_start = jnp.arange(n_tiles, dtype=jnp.int32) * tile
    tile_expert = jnp.minimum(jnp.sum(tile_start[:, None] >= ends[None, :], axis=1), n_exp - 1).astype(jnp.int32)
    n_used = (ends[n_exp - 1] // tile).astype(jnp.int32).reshape(1)
    return pos1, pos2, tile_expert, n_used


def _row_copies(jobs, n_rows):
    def copy(job, r):
        src_ref, dst_ref, sem, src_row, dst_row = job
        return pltpu.make_async_copy(src_ref.at[pl.ds(src_row(r), 1), :], dst_ref.at[pl.ds(dst_row(r), 1), :], sem)

    def start(r, carry):
        for job in jobs:
            copy(job, r).start()
        return carry

    def wait(r, carry):
        for job in jobs:
            copy(job, r).wait()
        return carry

    lax.fori_loop(0, n_rows, start, 0)
    lax.fori_loop(0, n_rows, wait, 0)


def _dispatch_body(pos1_ref, pos2_ref, x_ref, xs_in_hbm, xs_hbm, sem):
    del xs_in_hbm
    tm = x_ref.shape[0]
    local = lambda r: r
    _row_copies([(x_ref, xs_hbm, sem.at[0], local, lambda r: pos1_ref[0, r]),
                 (x_ref, xs_hbm, sem.at[1], local, lambda r: pos2_ref[0, r])], tm)


def _dispatch(x, pos1, pos2, n_rows, *, tm=256):
    m, d = x.shape
    tm = _tile(m, tm)
    pos_spec = pl.BlockSpec((None, 1, tm), lambda i: (i, 0, 0), memory_space=pltpu.SMEM)
    any_spec = pl.BlockSpec(memory_space=pl.ANY)
    return pl.pallas_call(
        _dispatch_body,
        grid=(m // tm,),
        in_specs=[pos_spec, pos_spec, pl.BlockSpec((tm, d), lambda i: (i, 0)), any_spec],
        out_specs=any_spec,
        out_shape=jax.ShapeDtypeStruct((n_rows, d), x.dtype),
        scratch_shapes=[pltpu.SemaphoreType.DMA((2,))],
        input_output_aliases={3: 0},
        compiler_params=_compiler_params(("arbitrary",), 2 * _nbytes((tm, d), x.dtype)),
        name="moe_dispatch",
    )(pos1.reshape(m // tm, 1, tm), pos2.reshape(m // tm, 1, tm), x, jnp.zeros((n_rows, d), x.dtype))


def _grouped_glu_body(te_ref, nu_ref, a_ref, wg_ref, wu_ref, o_ref, ab_ref):
    del te_ref
    used = pl.program_id(0) < nu_ref[0]

    @pl.when(used)
    def _():
        @pl.when(pl.program_id(1) == 0)
        def _():
            ab_ref[...] = a_ref[...].astype(BF16)

        a = ab_ref[...]
        g = jnp.dot(a, wg_ref[...], preferred_element_type=F32)
        u = jnp.dot(a, wu_ref[...], preferred_element_type=F32)
        o_ref[...] = (g * _sigmoid(g) * u).astype(o_ref.dtype)

    @pl.when(jnp.logical_not(used))
    def _():
        o_ref[...] = jnp.zeros_like(o_ref)


def _grouped_glu(xs, w_exp_in, layer, tile_expert, n_used, *, tile, tn=512):
    p, k = xs.shape
    f = w_exp_in.shape[3] // 2
    tn = _tile(f, tn)
    r = f // tn
    vmem = 2 * (_nbytes((tile, k), F32) + 2 * _nbytes((k, tn), BF16) + _nbytes((tile, tn), BF16))
    vmem += _nbytes((tile, k), BF16) + 3 * _nbytes((tile, tn), F32)
    return pl.pallas_call(
        _grouped_glu_body,
        grid_spec=pltpu.PrefetchScalarGridSpec(
            num_scalar_prefetch=2,
            grid=(p // tile, r),
            in_specs=[pl.BlockSpec((tile, k), lambda i, j, te, nu: (i, 0)),
                      pl.BlockSpec((None, None, k, tn), lambda i, j, te, nu: (layer, te[i], 0, j)),
                      pl.BlockSpec((None, None, k, tn), lambda i, j, te, nu: (layer, te[i], 0, j + r))],
            out_specs=pl.BlockSpec((tile, tn), lambda i, j, te, nu: (i, j)),
            scratch_shapes=[pltpu.VMEM((tile, k), BF16)]),
        out_shape=jax.ShapeDtypeStruct((p, f), BF16),
        compiler_params=_compiler_params(("arbitrary", "arbitrary"), vmem),
        name="expert_glu",
    )(tile_expert, n_used, xs, w_exp_in, w_exp_in)


def _grouped_out_body(te_ref, nu_ref, a_ref, w_ref, o_ref):
    del te_ref
    used = pl.program_id(0) < nu_ref[0]

    @pl.when(used)
    def _():
        o_ref[...] = jnp.dot(a_ref[...], w_ref[...], preferred_element_type=F32)

    @pl.when(jnp.logical_not(used))
    def _():
        o_ref[...] = jnp.zeros_like(o_ref)


def _grouped_out(act, w_exp_out, layer, tile_expert, n_used, *, tile):
    p, f = act.shape
    d = w_exp_out.shape[3]
    vmem = 2 * (_nbytes((tile, f), BF16) + _nbytes((f, d), BF16) + _nbytes((tile, d), F32))
    vmem += _nbytes((tile, d), F32)
    return pl.pallas_call(
        _grouped_out_body,
        grid_spec=pltpu.PrefetchScalarGridSpec(
            num_scalar_prefetch=2,
            grid=(p // tile,),
            in_specs=[pl.BlockSpec((tile, f), lambda i, te, nu: (i, 0)),
                      pl.BlockSpec((None, None, f, d), lambda i, te, nu: (layer, te[i], 0, 0))],
            out_specs=pl.BlockSpec((tile, d), lambda i, te, nu: (i, 0))),
        out_shape=jax.ShapeDtypeStruct((p, d), F32),
        compiler_params=_compiler_params(("arbitrary",), vmem),
        name="expert_out",
    )(tile_expert, n_used, act, w_exp_out)


def _combine_ln_body(pos1_ref, pos2_ref, x_ref, wts_ref, g_ref, b_ref, y_hbm, o_ref, ob_ref, y1_ref, y2_ref, sem,
                     *, alpha):
    tm, d = x_ref.shape
    local = lambda r: r
    _row_copies([(y_hbm, y1_ref, sem.at[0], lambda r: pos1_ref[0, r], local),
                 (y_hbm, y2_ref, sem.at[1], lambda r: pos2_ref[0, r], local)], tm)
    w1 = jnp.tile(wts_ref[:, :LANES], (1, d // LANES))
    w2 = jnp.tile(wts_ref[:, LANES:], (1, d // LANES))
    ff = w1 * y1_ref[...] + w2 * y2_ref[...]
    z = alpha * x_ref[...] + ff
    mu = jnp.mean(z, axis=-1, keepdims=True)
    zc = z - mu
    var = jnp.mean(zc * zc, axis=-1, keepdims=True)
    o = zc * lax.rsqrt(var + LN_EPS) * g_ref[...] + b_ref[...]
    o_ref[...] = o
    ob_ref[...] = o.astype(BF16)


def _combine_ln(x, y, pos1, pos2, wts, g, b, alpha, *, tm=256):
    m, d = x.shape
    tm = _tile(m, tm)
    row = pl.BlockSpec((tm, d), lambda i: (i, 0))
    vec = pl.BlockSpec((1, d), lambda i: (0, 0))
    pos_spec = pl.BlockSpec((None, 1, tm), lambda i: (i, 0, 0), memory_space=pltpu.SMEM)
    vmem = 2 * (2 * _nbytes((tm, d), F32) + _nbytes((tm, d), BF16) + _nbytes((tm, 2 * LANES), F32))
    vmem += 2 * _nbytes((tm, d), F32) + 4 * _nbytes((tm, d), F32)
    return pl.pallas_call(
        functools.partial(_combine_ln_body, alpha=alpha),
        grid=(m // tm,),
        in_specs=[pos_spec, pos_spec, row, pl.BlockSpec((tm, 2 * LANES), lambda i: (i, 0)), vec, vec,
                  pl.BlockSpec(memory_space=pl.ANY)],
        out_specs=[row, row],
        out_shape=[jax.ShapeDtypeStruct((m, d), F32), jax.ShapeDtypeStruct((m, d), BF16)],
        scratch_shapes=[pltpu.VMEM((tm, d), F32), pltpu.VMEM((tm, d), F32), pltpu.SemaphoreType.DMA((2,))],
        compiler_params=_compiler_params(("arbitrary",), vmem),
        name="moe_combine_ln",
    )(pos1.reshape(m // tm, 1, tm), pos2.reshape(m // tm, 1, tm), x, wts, g.reshape(1, d), b.reshape(1, d), y)


def kernel(x, w_in, b_f, conv_w, conv_b, w_a, b_a, w_x, b_x, lam, w_att_o, w_rnn_o, w_out,
           ln1_g, ln1_b, w_ffn_in, w_ffn_out, w_router, w_exp_in, w_exp_out, ln2_g, ln2_b):
    batch, seq, d_model = x.shape
    depth = w_in.shape[0]
    n_heads = b_f.shape[1]
    d_att = w_att_o.shape[1]
    d_rnn = w_rnn_o.shape[1]
    assert d_att == n_heads * LANES and w_a.shape[-1] == LANES
    assert w_in.shape[2] == 3 * d_att + n_heads + 2 * d_rnn + 2 * d_model
    alpha = float((2 * depth) ** 0.25)
    attn_blk = _tile(seq, 512)
    q_scale = float(LANES ** -0.5) * LOG2_E

    f_col0 = 3 * d_att
    g2_col0 = f_col0 + n_heads
    w_in_b = w_in.astype(BF16)
    w_f = w_in_b[:, :, f_col0:g2_col0]
    w_g2 = w_in_b[:, :, g2_col0:]
    w_a, w_x = w_a.astype(BF16), w_x.astype(BF16)
    w_att_o, w_rnn_o, w_out = w_att_o.astype(BF16), w_rnn_o.astype(BF16), w_out.astype(BF16)
    w_ffn_in, w_ffn_out = w_ffn_in.astype(BF16), w_ffn_out.astype(BF16)
    w_exp_in = w_exp_in.astype(BF16)
    w_exp_out = w_exp_out.astype(BF16)
    n_exp = w_router.shape[2]
    moe_rows = seq * TOP_K + n_exp * MOE_TILE

    outs = []
    for bi in range(batch):
        xf = x[bi]
        xb = xf.astype(BF16)
        for l in range(depth):
            qkv = _matmul(xb, w_in_b, l, BF16, n=f_col0, scaled_cols=d_att, col_scale=q_scale, name="proj_qkv")
            g2 = _matmul(xb, w_g2, l, F32, name="proj_rnn_gates")
            c = _decay_cumsum(xb, w_f[l], b_f[l], ts=attn_blk)
            c = c.reshape(n_heads, seq // attn_blk, attn_blk)
            att = _forgetting_attention(qkv, c, n_heads, blk=attn_blk)
            rnn = _rglru_branch(g2, d_rnn, conv_w[l], conv_b[l], w_a[l], b_a[l], w_x[l], b_x[l], lam[l])
            merged = _merge_branches(att, w_att_o, rnn, w_rnn_o, l, g2, 2 * d_rnn)
            mix = _matmul(merged, w_out, l, F32, name="proj_out")
            xf, xb = _deepnorm_ln(xf, mix, ln1_g[l], ln1_b[l], alpha)
            j = l // 2
            if l % 2 == 0:
                act = _dense_glu(xb, w_ffn_in, j)
                ff = _matmul(act, w_ffn_out, j, F32, tm=512, name="ffn_out")
                xf, xb = _deepnorm_ln(xf, ff, ln2_g[l], ln2_b[l], alpha)
            else:
                meta, wts, counts = _router(xf, w_router[j])
                pos1, pos2, tile_expert, n_used = _routing_tables(meta, counts, n_exp, MOE_TILE, moe_rows // MOE_TILE)
                xs = _dispatch(xf, pos1, pos2, moe_rows)
                act = _grouped_glu(xs, w_exp_in, j, tile_expert, n_used, tile=MOE_TILE)
                y = _grouped_out(act, w_exp_out, j, tile_expert, n_used, tile=MOE_TILE)
                xf, xb = _combine_ln(xf, y, pos1, pos2, wts, ln2_g[l], ln2_b[l], alpha)
        outs.append(xf)
    return jnp.stack(outs)
```

```python
import functools

import numpy as np
import jax
import jax.numpy as jnp
from jax import lax
from jax.experimental import pallas as pl
from jax.experimental.pallas import tpu as pltpu

F32 = jnp.float32
BF16 = jnp.bfloat16

LANES = 128
SUBLANES = 8
V7X_VMEM_BYTES = 64 * 2 ** 20
VMEM_RESERVE_BYTES = 8 * 2 ** 20

CONV_WIDTH = 4
RG_LRU_C = 8.0
TOP_K = 2
LN_EPS = 1e-5
NEG_INF = float("-inf")
LOG2_E = float(np.log2(np.e))
HEADS_PER_STEP = 2
MOE_TILE = 512


def _compiler_params(semantics, block_bytes):
    limit = min(int(block_bytes) + VMEM_RESERVE_BYTES, V7X_VMEM_BYTES - VMEM_RESERVE_BYTES)
    return pltpu.CompilerParams(dimension_semantics=semantics, vmem_limit_bytes=limit)


def _nbytes(shape, dtype):
    return int(np.prod(shape)) * jnp.dtype(dtype).itemsize


def _tile(dim, pref):
    if dim <= pref:
        return dim
    t = pref - pref % LANES
    while dim % t:
        t -= LANES
    assert t > 0, (dim, pref)
    return t


def _mm_body(a_ref, b_ref, o_ref, *, scaled_cols, col_scale):
    acc = jnp.dot(a_ref[...], b_ref[...], preferred_element_type=F32)
    if scaled_cols:
        acc = acc * jnp.where(pl.program_id(1) * o_ref.shape[1] < scaled_cols, col_scale, 1.0)
    o_ref[...] = acc.astype(o_ref.dtype)


def _matmul(a, b, layer, out_dtype, *, tm=1024, tn=512, col0=0, n=None, scaled_cols=0, col_scale=1.0, name):
    m, k = a.shape
    n = b.shape[2] - col0 if n is None else n
    tm, tn = _tile(m, tm), _tile(n, tn)
    assert scaled_cols % tn == 0 and col0 % tn == 0
    jb0 = col0 // tn
    vmem = 2 * (_nbytes((tm, k), a.dtype) + _nbytes((k, tn), b.dtype) + _nbytes((tm, tn), out_dtype))
    vmem += _nbytes((tm, tn), F32)
    return pl.pallas_call(
        functools.partial(_mm_body, scaled_cols=scaled_cols, col_scale=col_scale),
        grid=(m // tm, n // tn),
        in_specs=[pl.BlockSpec((tm, k), lambda i, j: (i, 0)),
                  pl.BlockSpec((None, k, tn), lambda i, j: (layer, 0, j + jb0))],
        out_specs=pl.BlockSpec((tm, tn), lambda i, j: (i, j)),
        out_shape=jax.ShapeDtypeStruct((m, n), out_dtype),
        compiler_params=_compiler_params(("parallel", "arbitrary"), vmem),
        name=name,
    )(a, b)


def _ln_body(x_ref, y_ref, g_ref, b_ref, o_ref, ob_ref, *, alpha):
    z = alpha * x_ref[...] + y_ref[...]
    mu = jnp.mean(z, axis=-1, keepdims=True)
    zc = z - mu
    var = jnp.mean(zc * zc, axis=-1, keepdims=True)
    y = zc * lax.rsqrt(var + LN_EPS)
    o = y * g_ref[...] + b_ref[...]
    o_ref[...] = o
    ob_ref[...] = o.astype(BF16)


def _deepnorm_ln(x, y, g, b, alpha, *, tm=256):
    m, d = x.shape
    tm = _tile(m, tm)
    row = pl.BlockSpec((tm, d), lambda i: (i, 0))
    vec = pl.BlockSpec((1, d), lambda i: (0, 0))
    vmem = 2 * (3 * _nbytes((tm, d), F32) + _nbytes((tm, d), BF16)) + 3 * _nbytes((tm, d), F32)
    return pl.pallas_call(
        functools.partial(_ln_body, alpha=alpha),
        grid=(m // tm,),
        in_specs=[row, row, vec, vec],
        out_specs=[row, row],
        out_shape=[jax.ShapeDtypeStruct((m, d), F32), jax.ShapeDtypeStruct((m, d), BF16)],
        compiler_params=_compiler_params(("parallel",), vmem),
        name="deepnorm_ln",
    )(x, y, g.reshape(1, d), b.reshape(1, d))


def _shift_rows(x, d, fill):
    rows = lax.broadcasted_iota(jnp.int32, x.shape, 0)
    return jnp.where(rows >= d, pltpu.roll(x, d, 0), fill)


def _linear_scan_rows(a, b, h0):
    sub = lax.broadcasted_iota(jnp.int32, a.shape, 0) % SUBLANES
    d = 1
    while d < SUBLANES:
        keep = sub >= d
        b = a * jnp.where(keep, pltpu.roll(b, d, 0), 0.0) + b
        a = a * jnp.where(keep, pltpu.roll(a, d, 0), 1.0)
        d *= 2
    groups = []
    carry = h0
    for g in range(a.shape[0] // SUBLANES):
        rows = slice(g * SUBLANES, (g + 1) * SUBLANES)
        h = b[rows] + a[rows] * carry
        groups.append(h)
        carry = h[SUBLANES - 1:SUBLANES]
    return jnp.concatenate(groups, axis=0)


def _cumsum_rows(x):
    d = 1
    while d < x.shape[0]:
        x = x + _shift_rows(x, d, 0.0)
        d *= 2
    return x


def _sigmoid(z):
    return 1.0 / (1.0 + jnp.exp(-z))


def _log_sigmoid(z):
    return jnp.minimum(z, 0.0) - jnp.log1p(jnp.exp(-jnp.abs(z)))


def _softplus(z):
    return jnp.maximum(z, 0.0) + jnp.log1p(jnp.exp(-jnp.abs(z)))


def _decay_body(x_ref, w_ref, bf_ref, o_ref, carry_ref):
    @pl.when(pl.program_id(0) == 0)
    def _():
        carry_ref[...] = jnp.zeros_like(carry_ref)

    z = jnp.dot(x_ref[...], w_ref[...], preferred_element_type=F32) + bf_ref[...]
    c = _cumsum_rows(_log_sigmoid(z)) + carry_ref[0:1, :]
    t = c.shape[0]
    carry_ref[0:1, :] = c[t - 1:t, :]
    o_ref[...] = (c * LOG2_E).T[:o_ref.shape[0], :]


def _decay_cumsum(xb, w_f, b_f, *, ts):
    s, d = xb.shape
    hp = max(SUBLANES, -(-b_f.shape[0] // SUBLANES) * SUBLANES)
    pad = LANES - b_f.shape[0]
    w_pad = jnp.pad(w_f, ((0, 0), (0, pad)))
    b_pad = jnp.pad(b_f, (0, pad)).reshape(1, LANES)
    vmem = 2 * (_nbytes((ts, d), BF16) + _nbytes((d, LANES), BF16) + _nbytes((hp, ts), F32))
    vmem += 6 * _nbytes((ts, LANES), F32)
    out = pl.pallas_call(
        _decay_body,
        grid=(s // ts,),
        in_specs=[pl.BlockSpec((ts, d), lambda t: (t, 0)),
                  pl.BlockSpec((d, LANES), lambda t: (0, 0)),
                  pl.BlockSpec((1, LANES), lambda t: (0, 0))],
        out_specs=pl.BlockSpec((hp, ts), lambda t: (0, t)),
        out_shape=jax.ShapeDtypeStruct((hp, s), F32),
        scratch_shapes=[pltpu.VMEM((SUBLANES, LANES), F32)],
        compiler_params=_compiler_params(("arbitrary",), vmem),
        name="decay_cumsum",
    )(xb, w_pad, b_pad)
    return out[:b_f.shape[0]]


def _attn_body(q_ref, k_ref, v_ref, c_ref, o_ref, m_ref, l_ref, acc_ref, p_ref, corr_ref, *, blk):
    i = pl.program_id(1)
    dh = LANES
    n_sub = q_ref.shape[1] // dh
    reps = blk // LANES
    ones_col = (lax.broadcasted_iota(jnp.int32, (blk, dh), 1) == 0).astype(BF16)

    def key_rows(j):
        return pl.ds(pl.multiple_of(j * blk, blk), blk)

    def scores(j, slot, masked=False):
        for a in range(n_sub):
            cols = slice(a * dh, (a + 1) * dh)
            s = lax.dot_general(q_ref[:, cols], k_ref[key_rows(j), cols], (((1,), (1,)), ((), ())),
                                preferred_element_type=F32)
            s = s - c_ref[a, pl.ds(j, 1), :]
            if masked:
                qpos = lax.broadcasted_iota(jnp.int32, s.shape, 0)
                kpos = lax.broadcasted_iota(jnp.int32, s.shape, 1)
                s = jnp.where(kpos <= qpos, s, NEG_INF)
            m_prev = m_ref[a]
            m_new = jnp.maximum(m_prev, jnp.max(s, axis=1, keepdims=True))
            p_ref[slot, a] = jnp.exp2(s - jnp.tile(m_new, (1, reps))).astype(BF16)
            corr_ref[slot, a] = jnp.exp2(m_prev - m_new)
            m_ref[a] = m_new

    def values(j, slot):
        for a in range(n_sub):
            v = v_ref[key_rows(j), a * dh:(a + 1) * dh]
            pv = jnp.dot(p_ref[slot, a], jnp.concatenate([v, ones_col], axis=1), preferred_element_type=F32)
            corr = corr_ref[slot, a]
            acc_ref[a] = corr * acc_ref[a] + pv[:, :dh]
            l_ref[a] = corr * l_ref[a] + pv[:, dh:]

    m_ref[...] = jnp.full_like(m_ref, NEG_INF)
    l_ref[...] = jnp.zeros_like(l_ref)
    acc_ref[...] = jnp.zeros_like(acc_ref)
    scores(i, 0, masked=True)
    n_pairs = i // 2

    def pair(u, carry):
        scores(2 * u, 1)
        values(jnp.where(u == 0, i, 2 * u - 1), 0)
        scores(2 * u + 1, 0)
        values(2 * u, 1)
        return carry

    lax.fori_loop(0, n_pairs, pair, 0)
    parked = jnp.where(n_pairs == 0, i, 2 * n_pairs - 1)

    @pl.when(i % 2 == 1)
    def _():
        scores(i - 1, 1)
        values(parked, 0)
        values(i - 1, 1)

    @pl.when(i % 2 == 0)
    def _():
        values(parked, 0)

    for a in range(n_sub):
        l = jnp.sum(l_ref[a], axis=1, keepdims=True)
        o_ref[:, a * dh:(a + 1) * dh] = (acc_ref[a] / l).astype(o_ref.dtype)


def _forgetting_attention(qkv, c, n_heads, *, blk):
    s = qkv.shape[0]
    dh = LANES
    nb = s // blk
    hps = HEADS_PER_STEP if n_heads % HEADS_PER_STEP == 0 else 1
    w = hps * dh
    groups = n_heads // hps
    head_cols = lambda off: pl.BlockSpec((s, w), lambda h, i: (0, off + h))
    vmem = 2 * (2 * _nbytes((s, w), BF16) + 2 * _nbytes((blk, w), BF16) + _nbytes((hps, nb, blk), F32))
    acc_like = pltpu.VMEM((hps, blk, dh), F32)
    p_slots = pltpu.VMEM((2, hps, blk, blk), BF16)
    corr_slots = pltpu.VMEM((2, hps, blk, dh), F32)
    vmem += 5 * _nbytes((hps, blk, dh), F32) + _nbytes((2, hps, blk, blk), BF16)
    vmem += 4 * hps * _nbytes((blk, blk), F32)
    return pl.pallas_call(
        functools.partial(_attn_body, blk=blk),
        grid=(groups, nb),
        in_specs=[pl.BlockSpec((blk, w), lambda h, i: (i, h)),
                  head_cols(groups), head_cols(2 * groups),
                  pl.BlockSpec((hps, nb, blk), lambda h, i: (h, 0, 0))],
        out_specs=pl.BlockSpec((blk, w), lambda h, i: (i, h)),
        out_shape=jax.ShapeDtypeStruct((s, n_heads * dh), BF16),
        scratch_shapes=[acc_like, acc_like, acc_like, p_slots, corr_slots],
        compiler_params=_compiler_params(("parallel", "arbitrary"), vmem),
        name="forgetting_attention",
    )(qkv, qkv, qkv, c)


def _gelu_tanh(x):
    z = np.float32(np.sqrt(2.0 / np.pi)) * (x + 0.044715 * (x * x * x))
    return x * _sigmoid(2.0 * z)


def _rglru_body(rx_ref, ry_ref, cw_ref, cb_ref, wa_ref, ba_ref, wx_ref, bx_ref, lam_ref,
                o_ref, tail_ref, h_ref):
    ts, tc = rx_ref.shape

    @pl.when(pl.program_id(1) == 0)
    def _():
        tail_ref[...] = jnp.zeros_like(tail_ref)
        h_ref[...] = jnp.zeros_like(h_ref)

    x = rx_ref[...]
    tail = tail_ref[...]
    sub = lax.broadcasted_iota(jnp.int32, tail.shape, 0)

    def delayed(k):
        rolled = pltpu.roll(x, k, 0)
        head = jnp.where(sub < k, pltpu.roll(tail, k, 0), rolled[:SUBLANES])
        return jnp.concatenate([head, rolled[SUBLANES:]], axis=0)

    xc = cb_ref[...]
    for j in range(CONV_WIDTH):
        k = CONV_WIDTH - 1 - j
        xc = xc + (delayed(k) if k else x) * cw_ref[j:j + 1, :]
    tail_ref[...] = x[ts - SUBLANES:, :]

    xcb = xc.astype(BF16)
    nblk = wa_ref.shape[0]
    rb = tc // nblk
    gate = lambda w_ref: jnp.concatenate(
        [jnp.dot(xcb[:, n * rb:(n + 1) * rb], w_ref[n], preferred_element_type=F32) for n in range(nblk)],
        axis=1)
    r = _sigmoid(gate(wa_ref) + ba_ref[...])
    i = _sigmoid(gate(wx_ref) + bx_ref[...])
    log_a = -RG_LRU_C * r * _softplus(-lam_ref[...])
    a = jnp.exp(log_a)
    u = (i * xc) * jnp.sqrt(-jnp.tanh(log_a) * (a * a + 1.0))
    h = _linear_scan_rows(a, u, h_ref[0:1, :])
    h_ref[0:1, :] = h[ts - 1:ts, :]
    o_ref[...] = (_gelu_tanh(ry_ref[...]) * h).astype(o_ref.dtype)


def _rglru_branch(g2, d_rnn, conv_w, conv_b, w_a, b_a, w_x, b_x, lam, *, ts=256, tc=512):
    s = g2.shape[0]
    ts, tc = _tile(s, ts), _tile(d_rnn, tc)
    rb = w_a.shape[-1]
    nblk = tc // rb
    ncb = d_rnn // tc
    vec = lambda v: v.reshape(1, d_rnn)
    vspec = pl.BlockSpec((1, tc), lambda c, t: (0, c))
    wspec = pl.BlockSpec((nblk, rb, rb), lambda c, t: (c, 0, 0))
    vmem = 2 * (2 * _nbytes((ts, tc), F32) + _nbytes((ts, tc), BF16) + 2 * _nbytes((nblk, rb, rb), BF16))
    vmem += 14 * _nbytes((ts, tc), F32)
    return pl.pallas_call(
        _rglru_body,
        grid=(ncb, s // ts),
        in_specs=[pl.BlockSpec((ts, tc), lambda c, t: (t, c)),
                  pl.BlockSpec((ts, tc), lambda c, t: (t, c + ncb)),
                  pl.BlockSpec((CONV_WIDTH, tc), lambda c, t: (0, c)), vspec,
                  wspec, vspec, wspec, vspec, vspec],
        out_specs=pl.BlockSpec((ts, tc), lambda c, t: (t, c)),
        out_shape=jax.ShapeDtypeStruct((s, d_rnn), BF16),
        scratch_shapes=[pltpu.VMEM((SUBLANES, tc), F32), pltpu.VMEM((SUBLANES, tc), F32)],
        compiler_params=_compiler_params(("parallel", "arbitrary"), vmem),
        name="rglru_branch",
    )(g2, g2, conv_w, vec(conv_b), w_a, vec(b_a), w_x, vec(b_x), vec(lam))


def _merge_body(att_ref, wo_a_ref, rnn_ref, wo_r_ref, ga_ref, gr_ref, o_ref):
    att = jnp.dot(att_ref[...], wo_a_ref[...], preferred_element_type=F32)
    rnn = jnp.dot(rnn_ref[...], wo_r_ref[...], preferred_element_type=F32)
    o_ref[...] = (_sigmoid(ga_ref[...]) * att + _sigmoid(gr_ref[...]) * rnn).astype(o_ref.dtype)


def _merge_branches(att, w_att_o, rnn, w_rnn_o, layer, g2, gate_col0, *, tm=1024, tn=512):
    m, ka = att.shape
    kr = rnn.shape[1]
    d = w_att_o.shape[2]
    tm, tn = _tile(m, tm), _tile(d, tn)
    ga0, gr0 = gate_col0 // tn, (gate_col0 + d) // tn
    vmem = 2 * (_nbytes((tm, ka + kr), BF16) + _nbytes((ka + kr, tn), BF16) + 2 * _nbytes((tm, tn), F32)
                + _nbytes((tm, tn), BF16)) + 3 * _nbytes((tm, tn), F32)
    return pl.pallas_call(
        _merge_body,
        grid=(m // tm, d // tn),
        in_specs=[pl.BlockSpec((tm, ka), lambda i, j: (i, 0)),
                  pl.BlockSpec((None, ka, tn), lambda i, j: (layer, 0, j)),
                  pl.BlockSpec((tm, kr), lambda i, j: (i, 0)),
                  pl.BlockSpec((None, kr, tn), lambda i, j: (layer, 0, j)),
                  pl.BlockSpec((tm, tn), lambda i, j: (i, j + ga0)),
                  pl.BlockSpec((tm, tn), lambda i, j: (i, j + gr0))],
        out_specs=pl.BlockSpec((tm, tn), lambda i, j: (i, j)),
        out_shape=jax.ShapeDtypeStruct((m, d), BF16),
        compiler_params=_compiler_params(("parallel", "arbitrary"), vmem),
        name="merge_branches",
    )(att, w_att_o, rnn, w_rnn_o, g2, g2)


def _glu_body(a_ref, wg_ref, wu_ref, o_ref):
    a = a_ref[...]
    g = jnp.dot(a, wg_ref[...], preferred_element_type=F32)
    u = jnp.dot(a, wu_ref[...], preferred_element_type=F32)
    o_ref[...] = (g * _sigmoid(g) * u).astype(o_ref.dtype)


def _glu_vmem(tm, k, tn):
    return 2 * (_nbytes((tm, k), BF16) + 2 * _nbytes((k, tn), BF16) + _nbytes((tm, tn), BF16)
                + _nbytes((tm, LANES), F32)) + 3 * _nbytes((tm, tn), F32)


def _dense_glu(xb, w_in, layer, *, tm=1024, tn=512):
    m, k = xb.shape
    f = w_in.shape[2] // 2
    tm, tn = _tile(m, tm), _tile(f, tn)
    nj = f // tn
    return pl.pallas_call(
        _glu_body,
        grid=(m // tm, nj),
        in_specs=[pl.BlockSpec((tm, k), lambda i, j: (i, 0)),
                  pl.BlockSpec((None, k, tn), lambda i, j: (layer, 0, j)),
                  pl.BlockSpec((None, k, tn), lambda i, j: (layer, 0, j + nj))],
        out_specs=pl.BlockSpec((tm, tn), lambda i, j: (i, j)),
        out_shape=jax.ShapeDtypeStruct((m, f), BF16),
        compiler_params=_compiler_params(("parallel", "arbitrary"), _glu_vmem(tm, k, tn)),
        name="dense_glu",
    )(xb, w_in, w_in)


def _router_body(x_ref, w_ref, meta_ref, wts_ref, cnt_ref, carry_ref, *, n_exp):
    @pl.when(pl.program_id(0) == 0)
    def _():
        carry_ref[...] = jnp.zeros_like(carry_ref)

    x, w = x_ref[...], w_ref[...]
    x_hi, w_hi = x.astype(BF16), w.astype(BF16)
    x_lo = (x - x_hi.astype(F32)).astype(BF16)
    w_lo = (w - w_hi.astype(F32)).astype(BF16)
    bdot = functools.partial(jnp.dot, preferred_element_type=F32)
    logits = bdot(x_hi, w_hi) + (bdot(x_lo, w_hi) + bdot(x_hi, w_lo))
    tm = logits.shape[0]
    lane = lax.broadcasted_iota(jnp.int32, logits.shape, 1)
    logits = jnp.where(lane < n_exp, logits, NEG_INF)

    def take_top(vals):
        top = jnp.max(vals, axis=1, keepdims=True)
        idx = jnp.min(jnp.where(vals == top, lane, LANES), axis=1, keepdims=True)
        return top, idx

    v1, i1 = take_top(logits)
    v2, i2 = take_top(jnp.where(lane == i1, NEG_INF, logits))
    e2 = jnp.exp(v2 - v1)
    denom = 1.0 + e2
    w1, w2 = 1.0 / denom, e2 / denom

    hot1 = (lane == i1).astype(F32)
    hot2 = (lane == i2).astype(F32)
    both = hot1 + hot2
    before = _cumsum_rows(both) - both + carry_ref[0:1, :]
    r1 = jnp.sum(hot1 * before, axis=1, keepdims=True).astype(jnp.int32)
    r2 = jnp.sum(hot2 * before, axis=1, keepdims=True).astype(jnp.int32)
    total = before[tm - 1:tm, :] + both[tm - 1:tm, :]
    carry_ref[0:1, :] = total
    cnt_ref[...] = jnp.broadcast_to(total, cnt_ref.shape)

    meta_ref[...] = jnp.where(lane == 0, i1, jnp.where(lane == 1, i2, jnp.where(lane == 2, r1,
                              jnp.where(lane == 3, r2, 0))))
    wts_ref[:, :LANES] = jnp.broadcast_to(w1, (tm, LANES))
    wts_ref[:, LANES:] = jnp.broadcast_to(w2, (tm, LANES))


def _router(x, w_router, *, tm=512):
    m, d = x.shape
    n_exp = w_router.shape[1]
    tm = _tile(m, tm)
    w_pad = jnp.pad(w_router, ((0, 0), (0, LANES - n_exp)))
    vmem = 2 * (_nbytes((tm, d), F32) + _nbytes((d, LANES), F32) + 3 * _nbytes((tm, LANES), F32))
    vmem += 16 * _nbytes((tm, LANES), F32) + 3 * _nbytes((tm, d), F32)
    return pl.pallas_call(
        functools.partial(_router_body, n_exp=n_exp),
        grid=(m // tm,),
        in_specs=[pl.BlockSpec((tm, d), lambda i: (i, 0)),
                  pl.BlockSpec((d, LANES), lambda i: (0, 0))],
        out_specs=[pl.BlockSpec((tm, LANES), lambda i: (i, 0)),
                   pl.BlockSpec((tm, 2 * LANES), lambda i: (i, 0)),
                   pl.BlockSpec((SUBLANES, LANES), lambda i: (0, 0))],
        out_shape=[jax.ShapeDtypeStruct((m, LANES), jnp.int32),
                   jax.ShapeDtypeStruct((m, 2 * LANES), F32),
                   jax.ShapeDtypeStruct((SUBLANES, LANES), F32)],
        scratch_shapes=[pltpu.VMEM((SUBLANES, LANES), F32)],
        compiler_params=_compiler_params(("arbitrary",), vmem),
        name="router_top2",
    )(x, w_pad)


def _routing_tables(meta, counts, n_exp, tile, n_tiles):
    e1, e2, r1, r2 = meta[:, 0], meta[:, 1], meta[:, 2], meta[:, 3]
    cnt = counts[0, :n_exp].astype(jnp.int32)
    padded = (cnt + tile - 1) // tile * tile
    ends = jnp.cumsum(padded)
    starts = ends - padded
    pos1 = starts[e1] + r1
    pos2 = starts[e2] + r2
    tile_start = jnp.arange(n_tiles, dtype=jnp.int32) * tile
    tile_expert = jnp.minimum(jnp.sum(tile_start[:, None] >= ends[None, :], axis=1), n_exp - 1).astype(jnp.int32)
    n_used = (ends[n_exp - 1] // tile).astype(jnp.int32).reshape(1)
    return pos1, pos2, tile_expert, n_used


def _row_copy(job, r):
    src_ref, dst_ref, sem, src_row, dst_row = job
    return pltpu.make_async_copy(src_ref.at[pl.ds(src_row(r), 1), :], dst_ref.at[pl.ds(dst_row(r), 1), :], sem)


def _start_row_copies(jobs, n_rows):
    def start(r, carry):
        for job in jobs:
            _row_copy(job, r).start()
        return carry

    lax.fori_loop(0, n_rows, start, 0)


def _wait_row_copies(jobs, n_rows):
    def wait(r, carry):
        for job in jobs:
            _row_copy(job, r).wait()
        return carry

    lax.fori_loop(0, n_rows, wait, 0)


def _row_copies(jobs, n_rows):
    _start_row_copies(jobs, n_rows)
    _wait_row_copies(jobs, n_rows)


def _dispatch_body(pos1_ref, pos2_ref, x_ref, xs_in_hbm, xs_hbm, sem):
    del xs_in_hbm
    tm = x_ref.shape[0]
    local = lambda r: r
    _row_copies([(x_ref, xs_hbm, sem.at[0], local, lambda r: pos1_ref[0, r]),
                 (x_ref, xs_hbm, sem.at[1], local, lambda r: pos2_ref[0, r])], tm)


def _dispatch(x, pos1, pos2, n_rows, *, tm=256):
    m, d = x.shape
    tm = _tile(m, tm)
    pos_spec = pl.BlockSpec((None, 1, tm), lambda i: (i, 0, 0), memory_space=pltpu.SMEM)
    any_spec = pl.BlockSpec(memory_space=pl.ANY)
    return pl.pallas_call(
        _dispatch_body,
        grid=(m // tm,),
        in_specs=[pos_spec, pos_spec, pl.BlockSpec((tm, d), lambda i: (i, 0)), any_spec],
        out_specs=any_spec,
        out_shape=jax.ShapeDtypeStruct((n_rows, d), x.dtype),
        scratch_shapes=[pltpu.SemaphoreType.DMA((2,))],
        input_output_aliases={3: 0},
        compiler_params=_compiler_params(("arbitrary",), 2 * _nbytes((tm, d), x.dtype)),
        name="moe_dispatch",
    )(pos1.reshape(m // tm, 1, tm), pos2.reshape(m // tm, 1, tm), x, jnp.zeros((n_rows, d), x.dtype))


def _grouped_glu_body(te_ref, nu_ref, a_ref, wg_ref, wu_ref, o_ref, ab_ref):
    del te_ref
    used = pl.program_id(0) < nu_ref[0]

    @pl.when(used)
    def _():
        @pl.when(pl.program_id(1) == 0)
        def _():
            ab_ref[...] = a_ref[...].astype(BF16)

        a = ab_ref[...]
        g = jnp.dot(a, wg_ref[...], preferred_element_type=F32)
        u = jnp.dot(a, wu_ref[...], preferred_element_type=F32)
        o_ref[...] = (g * _sigmoid(g) * u).astype(o_ref.dtype)

    @pl.when(jnp.logical_not(used))
    def _():
        o_ref[...] = jnp.zeros_like(o_ref)


def _grouped_glu(xs, w_exp_in, layer, tile_expert, n_used, *, tile, tn=512):
    p, k = xs.shape
    f = w_exp_in.shape[3] // 2
    tn = _tile(f, tn)
    r = f // tn
    vmem = 2 * (_nbytes((tile, k), F32) + 2 * _nbytes((k, tn), BF16) + _nbytes((tile, tn), BF16))
    vmem += _nbytes((tile, k), BF16) + 3 * _nbytes((tile, tn), F32)
    return pl.pallas_call(
        _grouped_glu_body,
        grid_spec=pltpu.PrefetchScalarGridSpec(
            num_scalar_prefetch=2,
            grid=(p // tile, r),
            in_specs=[pl.BlockSpec((tile, k), lambda i, j, te, nu: (i, 0)),
                      pl.BlockSpec((None, None, k, tn), lambda i, j, te, nu: (layer, te[i], 0, j)),
                      pl.BlockSpec((None, None, k, tn), lambda i, j, te, nu: (layer, te[i], 0, j + r))],
            out_specs=pl.BlockSpec((tile, tn), lambda i, j, te, nu: (i, j)),
            scratch_shapes=[pltpu.VMEM((tile, k), BF16)]),
        out_shape=jax.ShapeDtypeStruct((p, f), BF16),
        compiler_params=_compiler_params(("arbitrary", "arbitrary"), vmem),
        name="expert_glu",
    )(tile_expert, n_used, xs, w_exp_in, w_exp_in)


def _grouped_out_body(te_ref, nu_ref, a_ref, w_ref, o_ref):
    del te_ref
    used = pl.program_id(0) < nu_ref[0]

    @pl.when(used)
    def _():
        o_ref[...] = jnp.dot(a_ref[...], w_ref[...], preferred_element_type=F32)

    @pl.when(jnp.logical_not(used))
    def _():
        o_ref[...] = jnp.zeros_like(o_ref)


def _grouped_out(act, w_exp_out, layer, tile_expert, n_used, *, tile):
    p, f = act.shape
    d = w_exp_out.shape[3]
    vmem = 2 * (_nbytes((tile, f), BF16) + _nbytes((f, d), BF16) + _nbytes((tile, d), F32))
    vmem += _nbytes((tile, d), F32)
    return pl.pallas_call(
        _grouped_out_body,
        grid_spec=pltpu.PrefetchScalarGridSpec(
            num_scalar_prefetch=2,
            grid=(p // tile,),
            in_specs=[pl.BlockSpec((tile, f), lambda i, te, nu: (i, 0)),
                      pl.BlockSpec((None, None, f, d), lambda i, te, nu: (layer, te[i], 0, 0))],
            out_specs=pl.BlockSpec((tile, d), lambda i, te, nu: (i, 0))),
        out_shape=jax.ShapeDtypeStruct((p, d), F32),
        compiler_params=_compiler_params(("arbitrary",), vmem),
        name="expert_out",
    )(tile_expert, n_used, act, w_exp_out)


def _combine_ln_body(pos1_ref, pos2_ref, nxt1_ref, nxt2_ref, x_ref, wts_ref, g_ref, b_ref, y_hbm, o_ref, ob_ref,
                     ybuf_ref, sem, *, alpha):
    i = pl.program_id(0)
    tm, d = x_ref.shape
    slot = i % 2
    local = lambda r: r

    def gathers(s, p1_ref, p2_ref):
        return [(y_hbm, ybuf_ref.at[s, 0], sem.at[s, 0], lambda r: p1_ref[0, r], local),
                (y_hbm, ybuf_ref.at[s, 1], sem.at[s, 1], lambda r: p2_ref[0, r], local)]

    @pl.when(i == 0)
    def _():
        _start_row_copies(gathers(0, pos1_ref, pos2_ref), tm)

    @pl.when(i + 1 < pl.num_programs(0))
    def _():
        _start_row_copies(gathers(1 - slot, nxt1_ref, nxt2_ref), tm)

    _wait_row_copies(gathers(slot, pos1_ref, pos2_ref), tm)
    w1 = jnp.tile(wts_ref[:, :LANES], (1, d // LANES))
    w2 = jnp.tile(wts_ref[:, LANES:], (1, d // LANES))
    ff = w1 * ybuf_ref[slot, 0] + w2 * ybuf_ref[slot, 1]
    z = alpha * x_ref[...] + ff
    mu = jnp.mean(z, axis=-1, keepdims=True)
    zc = z - mu
    var = jnp.mean(zc * zc, axis=-1, keepdims=True)
    o = zc * lax.rsqrt(var + LN_EPS) * g_ref[...] + b_ref[...]
    o_ref[...] = o
    ob_ref[...] = o.astype(BF16)


def _combine_ln(x, y, pos1, pos2, wts, g, b, alpha, *, tm=128):
    m, d = x.shape
    tm = _tile(m, tm)
    row = pl.BlockSpec((tm, d), lambda i: (i, 0))
    vec = pl.BlockSpec((1, d), lambda i: (0, 0))
    n_steps = m // tm
    pos_spec = pl.BlockSpec((None, 1, tm), lambda i: (i, 0, 0), memory_space=pltpu.SMEM)
    nxt_spec = pl.BlockSpec((None, 1, tm), lambda i: (jnp.minimum(i + 1, n_steps - 1), 0, 0),
                            memory_space=pltpu.SMEM)
    vmem = 2 * (2 * _nbytes((tm, d), F32) + _nbytes((tm, d), BF16) + _nbytes((tm, 2 * LANES), F32))
    vmem += 4 * _nbytes((tm, d), F32) + 4 * _nbytes((tm, d), F32)
    p1, p2 = pos1.reshape(n_steps, 1, tm), pos2.reshape(n_steps, 1, tm)
    return pl.pallas_call(
        functools.partial(_combine_ln_body, alpha=alpha),
        grid=(n_steps,),
        in_specs=[pos_spec, pos_spec, nxt_spec, nxt_spec, row, pl.BlockSpec((tm, 2 * LANES), lambda i: (i, 0)),
                  vec, vec, pl.BlockSpec(memory_space=pl.ANY)],
        out_specs=[row, row],
        out_shape=[jax.ShapeDtypeStruct((m, d), F32), jax.ShapeDtypeStruct((m, d), BF16)],
        scratch_shapes=[pltpu.VMEM((2, 2, tm, d), F32), pltpu.SemaphoreType.DMA((2, 2))],
        compiler_params=_compiler_params(("arbitrary",), vmem),
        name="moe_combine_ln",
    )(p1, p2, p1, p2, x, wts, g.reshape(1, d), b.reshape(1, d), y)


def kernel(x, w_in, b_f, conv_w, conv_b, w_a, b_a, w_x, b_x, lam, w_att_o, w_rnn_o, w_out,
           ln1_g, ln1_b, w_ffn_in, w_ffn_out, w_router, w_exp_in, w_exp_out, ln2_g, ln2_b):
    batch, seq, d_model = x.shape
    depth = w_in.shape[0]
    n_heads = b_f.shape[1]
    d_att = w_att_o.shape[1]
    d_rnn = w_rnn_o.shape[1]
    assert d_att == n_heads * LANES and w_a.shape[-1] == LANES
    assert w_in.shape[2] == 3 * d_att + n_heads + 2 * d_rnn + 2 * d_model
    alpha = float((2 * depth) ** 0.25)
    attn_blk = _tile(seq, 512)
    q_scale = float(LANES ** -0.5) * LOG2_E

    f_col0 = 3 * d_att
    g2_col0 = f_col0 + n_heads
    w_in_b = w_in.astype(BF16)
    w_f = w_in_b[:, :, f_col0:g2_col0]
    w_g2 = w_in_b[:, :, g2_col0:]
    w_a, w_x = w_a.astype(BF16), w_x.astype(BF16)
    w_att_o, w_rnn_o, w_out = w_att_o.astype(BF16), w_rnn_o.astype(BF16), w_out.astype(BF16)
    w_ffn_in, w_ffn_out = w_ffn_in.astype(BF16), w_ffn_out.astype(BF16)
    w_exp_in = w_exp_in.astype(BF16)
    w_exp_out = w_exp_out.astype(BF16)
    n_exp = w_router.shape[2]
    moe_rows = seq * TOP_K + n_exp * MOE_TILE

    outs = []
    for bi in range(batch):
        xf = x[bi]
        xb = xf.astype(BF16)
        for l in range(depth):
            qkv = _matmul(xb, w_in_b, l, BF16, n=f_col0, scaled_cols=d_att, col_scale=q_scale, name="proj_qkv")
            g2 = _matmul(xb, w_g2, l, F32, name="proj_rnn_gates")
            c = _decay_cumsum(xb, w_f[l], b_f[l], ts=attn_blk)
            c = c.reshape(n_heads, seq // attn_blk, attn_blk)
            att = _forgetting_attention(qkv, c, n_heads, blk=attn_blk)
            rnn = _rglru_branch(g2, d_rnn, conv_w[l], conv_b[l], w_a[l], b_a[l], w_x[l], b_x[l], lam[l])
            merged = _merge_branches(att, w_att_o, rnn, w_rnn_o, l, g2, 2 * d_rnn)
            mix = _matmul(merged, w_out, l, F32, name="proj_out")
            xf, xb = _deepnorm_ln(xf, mix, ln1_g[l], ln1_b[l], alpha)
            j = l // 2
            if l % 2 == 0:
                act = _dense_glu(xb, w_ffn_in, j)
                ff = _matmul(act, w_ffn_out, j, F32, tm=512, name="ffn_out")
                xf, xb = _deepnorm_ln(xf, ff, ln2_g[l], ln2_b[l], alpha)
            else:
                meta, wts, counts = _router(xf, w_router[j])
                pos1, pos2, tile_expert, n_used = _routing_tables(meta, counts, n_exp, MOE_TILE, moe_rows // MOE_TILE)
                xs = _dispatch(xf, pos1, pos2, moe_rows)
                act = _grouped_glu(xs, w_exp_in, j, tile_expert, n_used, tile=MOE_TILE)
                y = _grouped_out(act, w_exp_out, j, tile_expert, n_used, tile=MOE_TILE)
                xf, xb = _combine_ln(xf, y, pos1, pos2, wts, ln2_g[l], ln2_b[l], alpha)
        outs.append(xf)
    return jnp.stack(outs)
```
